```python
import jax, jax.numpy as jnp
from jax import lax
import numpy as np

D_MODEL = 2048
BATCH = 4
SEQ = 2048
DEPTH = 1
DEC_BATCH = 128
DEC_SEQ = 4
PAST_LEN = 16384
PAGE_SIZE = 128

D_POOL = D_MODEL
POOL_WINDOWS = (2, 4, 8, 16)
N_POOL_GROUPS = len(POOL_WINDOWS)
POOL_GROUP = D_POOL // N_POOL_GROUPS
POOL_BUF = max(POOL_WINDOWS) - 1
EXPAND = 2
D_INNER = EXPAND * D_MODEL
HEAD_DIM = 64
N_HEADS = D_INNER // HEAD_DIM
D_STATE = 128
N_GROUPS = 8
HEADS_PER_GROUP = N_HEADS // N_GROUPS
CONV_W = 4
CONV_DIM = D_INNER + 2 * N_GROUPS * D_STATE
CHUNK = 128
D_FF = 256 * ((8 * D_MODEL // 3 + 255) // 256)
N_MOD = 9
EPS = 1e-6
D_IN_PROJ = D_POOL + D_INNER + CONV_DIM + N_HEADS + 2 * D_MODEL
IN_SPLITS = (D_POOL, D_POOL + D_INNER, D_POOL + D_INNER + CONV_DIM,
             D_POOL + D_INNER + CONV_DIM + N_HEADS, D_POOL + D_INNER + CONV_DIM + N_HEADS + D_MODEL)

kernel_name = "pool_ssd_gated_macaron_adaln_step"


def rms_norm(x, g):
    xf = x.astype(jnp.float32)
    y = xf * lax.rsqrt(jnp.mean(xf * xf, axis=-1, keepdims=True) + EPS)
    return (y * g.astype(jnp.float32)).astype(x.dtype)


def modulate(x, shift, scale):
    return x * (1 + scale[:, None, :]) + shift[:, None, :]


def swiglu(x, w13, w2):
    a, b = jnp.split(x @ w13, 2, axis=-1)
    return (jax.nn.silu(a) * b) @ w2


def pool_mixer(u, buf, n_prev, w_group, scale):
    b, L, _ = u.shape
    full_raw = jnp.concatenate([buf.astype(u.dtype), u], axis=1)
    full = full_raw.astype(jnp.float32)
    cs = jnp.concatenate([jnp.zeros((b, 1, D_POOL), jnp.float32), jnp.cumsum(full, axis=1)], axis=1)
    end = cs[:, POOL_BUF + 1:]
    t = jnp.arange(L)
    outs = []
    for gi, w in enumerate(POOL_WINDOWS):
        lo, hi = gi * POOL_GROUP, (gi + 1) * POOL_GROUP
        start = cs[:, POOL_BUF + 1 - w: POOL_BUF + 1 - w + L, lo:hi]
        cnt = jnp.minimum(w, t + 1 + n_prev).astype(jnp.float32)
        outs.append((end[..., lo:hi] - start) / cnt[None, :, None])
    pooled = (jnp.concatenate(outs, axis=-1) - u.astype(jnp.float32)).astype(u.dtype)
    pg = pooled.reshape(b, L, N_POOL_GROUPS, POOL_GROUP)
    mixed = jnp.einsum('blgc,gcd->blgd', pg, w_group).reshape(b, L, D_POOL)
    return mixed * scale, full_raw[:, -POOL_BUF:]


def causal_conv(xs, buf, w, bias):
    L = xs.shape[1]
    full = jnp.concatenate([buf.astype(xs.dtype), xs], axis=1)
    out = bias + full[:, 0:L] * w[0]
    for k in range(1, CONV_W):
        out = out + full[:, k:k + L] * w[k]
    return out, full[:, -(CONV_W - 1):]


def ssd_scan(x, dt, a, bm, cm, h0):
    b, L = x.shape[:2]
    q = CHUNK if L % CHUNK == 0 else L
    nc = L // q
    f32 = jnp.float32
    xr = x.astype(f32).reshape(b, nc, q, N_GROUPS, HEADS_PER_GROUP, HEAD_DIM)
    dtr = dt.astype(f32).reshape(b, nc, q, N_GROUPS, HEADS_PER_GROUP)
    lar = dtr * a.reshape(N_GROUPS, HEADS_PER_GROUP)
    br = bm.astype(f32).reshape(b, nc, q, N_GROUPS, D_STATE)
    cr = cm.astype(f32).reshape(b, nc, q, N_GROUPS, D_STATE)
    seqs = tuple(jnp.moveaxis(v, 1, 0) for v in (xr, dtr, lar, br, cr))
    causal = jnp.tril(jnp.ones((q, q), dtype=bool))[None, :, :, None, None]

    def step(h, inp):
        xc, dtc, lac, bc, cc = inp
        s = jnp.cumsum(lac, axis=1)
        diff = s[:, :, None] - s[:, None, :]
        decay = jnp.exp(jnp.where(causal, diff, -jnp.inf))
        cb = jnp.einsum('btgn,bsgn->btsg', cc, bc)
        wts = cb[..., None] * decay * dtc[:, None]
        y = jnp.einsum('btsge,bsgep->btgep', wts, xc)
        y = y + jnp.einsum('btgn,bgepn->btgep', cc, h) * jnp.exp(s)[..., None]
        tail = jnp.exp(s[:, -1:] - s) * dtc
        h = h * jnp.exp(s[:, -1])[..., None, None] + jnp.einsum('bsge,bsgep,bsgn->bgepn', tail, xc, bc)
        return h, y

    h0r = h0.astype(f32).reshape(b, N_GROUPS, HEADS_PER_GROUP, HEAD_DIM, D_STATE)
    hT, ys = lax.scan(step, h0r, seqs)
    y = jnp.moveaxis(ys, 0, 1).reshape(b, L, N_HEADS, HEAD_DIM)
    return y, hT.reshape(b, N_HEADS, HEAD_DIM, D_STATE)


def mamba_branch(xbc_in, z, dt_raw, conv_buf, ssm_state, conv_w, conv_b, dt_bias, a_log, d_skip, ssm_norm):
    b, L, _ = z.shape
    xbc, new_conv = causal_conv(xbc_in, conv_buf, conv_w, conv_b)
    xbc = jax.nn.silu(xbc)
    xh, bm, cm = jnp.split(xbc, [D_INNER, D_INNER + N_GROUPS * D_STATE], axis=-1)
    xh = xh.reshape(b, L, N_HEADS, HEAD_DIM)
    bm = bm.reshape(b, L, N_GROUPS, D_STATE)
    cm = cm.reshape(b, L, N_GROUPS, D_STATE)
    dt = jax.nn.softplus(dt_raw.astype(jnp.float32) + dt_bias.astype(jnp.float32))
    a = -jnp.exp(a_log.astype(jnp.float32))
    y, h = ssd_scan(xh, dt, a, bm, cm, ssm_state)
    y = y + xh.astype(jnp.float32) * d_skip.astype(jnp.float32)[:, None]
    y = y.reshape(b, L, D_INNER) * jax.nn.silu(z.astype(jnp.float32))
    yg = y.reshape(b, L, N_GROUPS, D_INNER // N_GROUPS)
    yg = yg * lax.rsqrt(jnp.mean(yg * yg, axis=-1, keepdims=True) + EPS)
    y = (yg.reshape(b, L, D_INNER) * ssm_norm.astype(jnp.float32)).astype(z.dtype)
    return y, new_conv, h


def trunk(x, c, pool_bufs, n_prev, conv_bufs, ssm_states, params):
    (w_ada, b_ada, norm_ffn1, w13_ffn1, w2_ffn1, norm_mix, w_in, pool_w, pool_scale,
     conv_w, conv_b, dt_bias, a_log, d_skip, ssm_norm, w_branch_pool, w_branch_ssm, w_out,
     norm_ffn2, w13_ffn2, w2_ffn2, norm_final) = params
    h = x
    new_ssm, new_conv, new_pool = [], [], []
    for l in range(DEPTH):
        mods = jax.nn.silu(c) @ w_ada[l] + b_ada[l]
        sh1, sc1, g1, sh2, sc2, g2, sh3, sc3, g3 = jnp.split(mods, N_MOD, axis=-1)
        a1 = modulate(rms_norm(h, norm_ffn1[l]), sh1, sc1)
        h = h + 0.5 * g1[:, None] * swiglu(a1, w13_ffn1[l], w2_ffn1[l])
        u = modulate(rms_norm(h, norm_mix[l]), sh2, sc2)
        proj = u @ w_in[l]
        u_pool, z, xbc_in, dt_raw, gate_pool, gate_ssm = jnp.split(proj, IN_SPLITS, axis=-1)
        y_pool, pb = pool_mixer(u_pool, pool_bufs[l], n_prev, pool_w[l], pool_scale[l])
        y_ssm, cbuf, hs = mamba_branch(xbc_in, z, dt_raw, conv_bufs[l], ssm_states[l], conv_w[l], conv_b[l],
                                       dt_bias[l], a_log[l], d_skip[l], ssm_norm[l])
        merged = (jax.nn.sigmoid(gate_pool) * (y_pool @ w_branch_pool[l])
                  + jax.nn.sigmoid(gate_ssm) * (y_ssm @ w_branch_ssm[l]))
        h = h + g2[:, None] * (merged @ w_out[l])
        a3 = modulate(rms_norm(h, norm_ffn2[l]), sh3, sc3)
        h = h + 0.5 * g3[:, None] * swiglu(a3, w13_ffn2[l], w2_ffn2[l])
        new_ssm.append(hs)
        new_conv.append(cbuf)
        new_pool.append(pb)
    return rms_norm(h, norm_final), jnp.stack(new_ssm), jnp.stack(new_conv), jnp.stack(new_pool)


def setup_inputs(seed: int = 0) -> dict:
    key = jax.random.key(seed)
    ks = jax.random.split(key, 40)
    n = lambda k, s, sc: jax.random.normal(k, s, jnp.float32) * sc
    ones = lambda k, s: 1.0 + 0.02 * jax.random.normal(k, s, jnp.float32)
    dt0 = jnp.exp(jax.random.uniform(ks[30], (DEPTH, N_HEADS)) * (np.log(0.1) - np.log(0.001)) + np.log(0.001))
    return {
        "x_prompt": n(ks[0], (BATCH, SEQ, D_MODEL), 1.0),
        "x_sample": n(ks[1], (DEC_BATCH, DEC_SEQ, D_MODEL), 1.0),
        "c_prompt": n(ks[2], (BATCH, D_MODEL), 1.0),
        "c_sample": n(ks[3], (DEC_BATCH, D_MODEL), 1.0),
        "state_ssm": n(ks[4], (DEPTH, DEC_BATCH, N_HEADS, HEAD_DIM, D_STATE), 0.1),
        "state_conv": n(ks[5], (DEPTH, DEC_BATCH, CONV_W - 1, CONV_DIM), 1.0),
        "state_pool": n(ks[6], (DEPTH, DEC_BATCH, POOL_BUF, D_POOL), 1.0),
        "w_ada": n(ks[7], (DEPTH, D_MODEL, N_MOD * D_MODEL), 0.5 * D_MODEL ** -0.5),
        "b_ada": n(ks[8], (DEPTH, N_MOD * D_MODEL), 0.01),
        "norm_ffn1": ones(ks[9], (DEPTH, D_MODEL)),
        "w13_ffn1": n(ks[10], (DEPTH, D_MODEL, 2 * D_FF), D_MODEL ** -0.5),
        "w2_ffn1": n(ks[11], (DEPTH, D_FF, D_MODEL), D_FF ** -0.5),
        "norm_mix": ones(ks[12], (DEPTH, D_MODEL)),
        "w_in": n(ks[13], (DEPTH, D_MODEL, D_IN_PROJ), D_MODEL ** -0.5),
        "pool_w": n(ks[14], (DEPTH, N_POOL_GROUPS, POOL_GROUP, POOL_GROUP), POOL_GROUP ** -0.5),
        "pool_scale": ones(ks[15], (DEPTH, D_POOL)),
        "conv_w": n(ks[16], (DEPTH, CONV_W, CONV_DIM), CONV_W ** -0.5),
        "conv_b": n(ks[17], (DEPTH, CONV_DIM), 0.01),
        "dt_bias": dt0 + jnp.log(-jnp.expm1(-dt0)),
        "a_log": jnp.log(jax.random.uniform(ks[18], (DEPTH, N_HEADS), jnp.float32, 1.0, 16.0)),
        "d_skip": ones(ks[19], (DEPTH, N_HEADS)),
        "ssm_norm": ones(ks[20], (DEPTH, D_INNER)),
        "w_branch_pool": n(ks[21], (DEPTH, D_POOL, D_MODEL), D_POOL ** -0.5),
        "w_branch_ssm": n(ks[22], (DEPTH, D_INNER, D_MODEL), D_INNER ** -0.5),
        "w_out": n(ks[23], (DEPTH, D_MODEL, D_MODEL), D_MODEL ** -0.5),
        "norm_ffn2": ones(ks[24], (DEPTH, D_MODEL)),
        "w13_ffn2": n(ks[25], (DEPTH, D_MODEL, 2 * D_FF), D_MODEL ** -0.5),
        "w2_ffn2": n(ks[26], (DEPTH, D_FF, D_MODEL), D_FF ** -0.5),
        "norm_final": ones(ks[27], (D_MODEL,)),
    }


def reference(x_prompt, x_sample, c_prompt, c_sample, state_ssm, state_conv, state_pool,
              w_ada, b_ada, norm_ffn1, w13_ffn1, w2_ffn1, norm_mix, w_in, pool_w, pool_scale,
              conv_w, conv_b, dt_bias, a_log, d_skip, ssm_norm, w_branch_pool, w_branch_ssm, w_out,
              norm_ffn2, w13_ffn2, w2_ffn2, norm_final):
    params = (w_ada, b_ada, norm_ffn1, w13_ffn1, w2_ffn1, norm_mix, w_in, pool_w, pool_scale,
              conv_w, conv_b, dt_bias, a_log, d_skip, ssm_norm, w_branch_pool, w_branch_ssm, w_out,
              norm_ffn2, w13_ffn2, w2_ffn2, norm_final)
    b = x_prompt.shape[0]
    zero_pool = jnp.zeros((DEPTH, b, POOL_BUF, D_POOL), x_prompt.dtype)
    zero_conv = jnp.zeros((DEPTH, b, CONV_W - 1, CONV_DIM), x_prompt.dtype)
    zero_ssm = jnp.zeros((DEPTH, b, N_HEADS, HEAD_DIM, D_STATE), jnp.float32)
    y_prompt, p_ssm, p_conv, p_pool = trunk(x_prompt, c_prompt, zero_pool, 0, zero_conv, zero_ssm, params)
    y_sample, s_ssm, s_conv, s_pool = trunk(x_sample, c_sample, state_pool, min(PAST_LEN, POOL_BUF),
                                            state_conv, state_ssm, params)
    return (y_prompt, y_sample, p_ssm, p_conv, p_pool, s_ssm, s_conv, s_pool)
```

```python
import functools

import jax
import jax.numpy as jnp
from jax import lax
from jax.experimental import pallas as pl
from jax.experimental.pallas import tpu as pltpu

F32 = jnp.float32
BF16 = jnp.bfloat16

D_MODEL = 2048
BATCH = 4
SEQ = 2048
DEC_BATCH = 128
DEC_SEQ = 4
PAST_LEN = 16384
POOL_WINDOWS = (2, 4, 8, 16)
POOL_GROUP = D_MODEL // len(POOL_WINDOWS)
POOL_BUF = max(POOL_WINDOWS) - 1
D_INNER = 2 * D_MODEL
HEAD_DIM = 64
N_HEADS = D_INNER // HEAD_DIM
D_STATE = 128
N_GROUPS = 8
HEADS_PER_GROUP = N_HEADS // N_GROUPS
GROUP_CH = D_INNER // N_GROUPS
CONV_W = 4
CONV_DIM = D_INNER + 2 * N_GROUPS * D_STATE
CHUNK = 128
D_FF = 256 * ((8 * D_MODEL // 3 + 255) // 256)
N_MOD = 9
EPS = 1e-6

N_PROMPT = BATCH * SEQ
N_SAMPLE = DEC_BATCH * DEC_SEQ
N_ROWS = N_PROMPT + N_SAMPLE
ROW_TILE = 512
N_ROW_TILES = N_ROWS // ROW_TILE
PROMPT_TILES = N_PROMPT // ROW_TILE
TILES_PER_SEQ = SEQ // ROW_TILE
N_CHUNKS = SEQ // CHUNK
LANES = 128
PAIRS_PER_GROUP = HEADS_PER_GROUP // 2
SAMPLE_BB = 32
XBC_B_COL = D_INNER // D_STATE
XBC_C_COL = XBC_B_COL + N_GROUPS
MIB = 1024 * 1024


def _params(n_axes, vmem_mib):
    return pltpu.CompilerParams(dimension_semantics=("arbitrary",) * n_axes,
                                vmem_limit_bytes=vmem_mib * MIB)


def _silu(x):
    return x * jax.nn.sigmoid(x)


def _row_mods(mp_ref, ms_ref, tile):
    seq = jnp.minimum(tile // TILES_PER_SEQ, BATCH - 1)
    return jnp.where(tile >= PROMPT_TILES, ms_ref[...], mp_ref[pl.ds(seq, 1), :])


def _ada_kernel(c_ref, w_ref, b_ref, o_ref):
    c = c_ref[...]
    a = _silu(c).astype(BF16)
    o_ref[...] = jnp.dot(a, w_ref[...].astype(BF16), preferred_element_type=F32) + b_ref[...]


def _ada(c_all, w, b):
    m = c_all.shape[0]
    n = w.shape[1]
    tn = 1024
    return pl.pallas_call(
        _ada_kernel,
        grid=(n // tn,),
        in_specs=[pl.BlockSpec((m, D_MODEL), lambda j: (0, 0)),
                  pl.BlockSpec((D_MODEL, tn), lambda j: (0, j)),
                  pl.BlockSpec((1, tn), lambda j: (0, j))],
        out_specs=pl.BlockSpec((m, tn), lambda j: (0, j)),
        out_shape=jax.ShapeDtypeStruct((m, n), F32),
        compiler_params=_params(1, 40),
        name="ada_mods",
    )(c_all, w, b)


def _normmod_kernel(h_ref, g_ref, shp_ref, shs_ref, scp_ref, scs_ref, o_ref):
    i = pl.program_id(0)
    x = h_ref[...]
    y = x * lax.rsqrt(jnp.mean(x * x, axis=-1, keepdims=True) + EPS) * g_ref[...]
    shift = _row_mods(shp_ref, shs_ref, i)
    scale = _row_mods(scp_ref, scs_ref, i)
    o_ref[...] = (y * (1.0 + scale) + shift).astype(BF16)


def _normmod(h, g, mods_p, mods_s, shift_chunk, scale_chunk):
    return pl.pallas_call(
        _normmod_kernel,
        grid=(N_ROW_TILES,),
        in_specs=[pl.BlockSpec((ROW_TILE, D_MODEL), lambda i: (i, 0)),
                  pl.BlockSpec((1, D_MODEL), lambda i: (0, 0)),
                  pl.BlockSpec((BATCH, D_MODEL), lambda i: (0, shift_chunk)),
                  pl.BlockSpec((ROW_TILE, D_MODEL), lambda i: (0, shift_chunk)),
                  pl.BlockSpec((BATCH, D_MODEL), lambda i: (0, scale_chunk)),
                  pl.BlockSpec((ROW_TILE, D_MODEL), lambda i: (0, scale_chunk))],
        out_specs=pl.BlockSpec((ROW_TILE, D_MODEL), lambda i: (i, 0)),
        out_shape=jax.ShapeDtypeStruct((N_ROWS, D_MODEL), BF16),
        compiler_params=_params(1, 48),
        name="norm_modulate",
    )(h, g.reshape(1, D_MODEL), mods_p, mods_s, mods_p, mods_s)


def _final_norm_kernel(h_ref, g_ref, o_ref):
    x = h_ref[...]
    o_ref[...] = x * lax.rsqrt(jnp.mean(x * x, axis=-1, keepdims=True) + EPS) * g_ref[...]


def _final_norm(h, g):
    return pl.pallas_call(
        _final_norm_kernel,
        grid=(N_ROW_TILES,),
        in_specs=[pl.BlockSpec((ROW_TILE, D_MODEL), lambda i: (i, 0)),
                  pl.BlockSpec((1, D_MODEL), lambda i: (0, 0))],
        out_specs=pl.BlockSpec((ROW_TILE, D_MODEL), lambda i: (i, 0)),
        out_shape=jax.ShapeDtypeStruct((N_ROWS, D_MODEL), F32),
        compiler_params=_params(1, 40),
        name="final_norm",
    )(h, g.reshape(1, D_MODEL))


def _ffn_up_kernel(a_ref, wa_ref, wb_ref, o_ref, wa_s, wb_s):
    @pl.when(pl.program_id(1) == 0)
    def _():
        wa_s[...] = wa_ref[...].astype(BF16)
        wb_s[...] = wb_ref[...].astype(BF16)

    a = a_ref[...]
    ha = jnp.dot(a, wa_s[...], preferred_element_type=F32)
    hb = jnp.dot(a, wb_s[...], preferred_element_type=F32)
    o_ref[...] = (_silu(ha) * hb).astype(BF16)


def _ffn_up(a, w13):
    tn = 512
    nt = D_FF // tn
    return pl.pallas_call(
        _ffn_up_kernel,
        grid=(nt, N_ROW_TILES),
        in_specs=[pl.BlockSpec((ROW_TILE, D_MODEL), lambda j, i: (i, 0)),
                  pl.BlockSpec((D_MODEL, tn), lambda j, i: (0, j)),
                  pl.BlockSpec((D_MODEL, tn), lambda j, i: (0, nt + j))],
        out_specs=pl.BlockSpec((ROW_TILE, tn), lambda j, i: (i, j)),
        out_shape=jax.ShapeDtypeStruct((N_ROWS, D_FF), BF16),
        scratch_shapes=[pltpu.VMEM((D_MODEL, tn), BF16), pltpu.VMEM((D_MODEL, tn), BF16)],
        compiler_params=_params(2, 48),
        name="ffn_up",
    )(a, w13, w13)


def _mm_res_kernel(a_ref, w_ref, res_ref, gp_ref, gs_ref, o_ref, w_s, *, scale):
    @pl.when(pl.program_id(1) == 0)
    def _():
        w_s[...] = w_ref[...].astype(BF16)

    gate = _row_mods(gp_ref, gs_ref, pl.program_id(1))
    acc = jnp.dot(a_ref[...], w_s[...], preferred_element_type=F32)
    if scale != 1.0:
        gate = scale * gate
    o_ref[...] = res_ref[...] + gate * acc


def _mm_res(a, w, res, mods_p, mods_s, gate_chunk, scale, tn):
    k = a.shape[1]
    nt = D_MODEL // tn
    return pl.pallas_call(
        functools.partial(_mm_res_kernel, scale=scale),
        grid=(nt, N_ROW_TILES),
        in_specs=[pl.BlockSpec((ROW_TILE, k), lambda j, i: (i, 0)),
                  pl.BlockSpec((k, tn), lambda j, i: (0, j)),
                  pl.BlockSpec((ROW_TILE, tn), lambda j, i: (i, j)),
                  pl.BlockSpec((BATCH, tn), lambda j, i: (0, gate_chunk * nt + j)),
                  pl.BlockSpec((ROW_TILE, tn), lambda j, i: (0, gate_chunk * nt + j))],
        out_specs=pl.BlockSpec((ROW_TILE, tn), lambda j, i: (i, j)),
        out_shape=jax.ShapeDtypeStruct((N_ROWS, D_MODEL), F32),
        scratch_shapes=[pltpu.VMEM((k, tn), BF16)],
        compiler_params=_params(2, 48),
        name="mm_gated_residual",
    )(a, w, res, mods_p, mods_s)


def _mm_plain_kernel(a_ref, w_ref, o_ref, w_s):
    @pl.when(pl.program_id(1) == 0)
    def _():
        w_s[...] = w_ref[...].astype(BF16)

    o_ref[...] = jnp.dot(a_ref[...], w_s[...], preferred_element_type=F32)


def _mm_plain(a, w, col0, ncols, tn):
    k = a.shape[1]
    nt = ncols // tn
    off = col0 // tn
    return pl.pallas_call(
        _mm_plain_kernel,
        grid=(nt, N_ROW_TILES),
        in_specs=[pl.BlockSpec((ROW_TILE, k), lambda j, i: (i, 0)),
                  pl.BlockSpec((k, tn), lambda j, i: (0, off + j))],
        out_specs=pl.BlockSpec((ROW_TILE, tn), lambda j, i: (i, j)),
        out_shape=jax.ShapeDtypeStruct((N_ROWS, ncols), F32),
        scratch_shapes=[pltpu.VMEM((k, tn), BF16)],
        compiler_params=_params(2, 40),
        name="mm_in_proj",
    )(a, w)


def _merge_kernel(yp_ref, ys_ref, wp_ref, ws_ref, gp_ref, gs_ref, o_ref, wp_s, ws_s):
    @pl.when(pl.program_id(1) == 0)
    def _():
        wp_s[...] = wp_ref[...].astype(BF16)
        ws_s[...] = ws_ref[...].astype(BF16)

    mp = jnp.dot(yp_ref[...], wp_s[...], preferred_element_type=F32)
    ms = jnp.dot(ys_ref[...], ws_s[...], preferred_element_type=F32)
    o_ref[...] = (jax.nn.sigmoid(gp_ref[...]) * mp + jax.nn.sigmoid(gs_ref[...]) * ms).astype(BF16)


def _merge(y_pool, y_ssm, w_bp, w_bs, gates):
    tn = 256
    nt = D_MODEL // tn
    return pl.pallas_call(
        _merge_kernel,
        grid=(nt, N_ROW_TILES),
        in_specs=[pl.BlockSpec((ROW_TILE, D_MODEL), lambda j, i: (i, 0)),
                  pl.BlockSpec((ROW_TILE, D_INNER), lambda j, i: (i, 0)),
                  pl.BlockSpec((D_MODEL, tn), lambda j, i: (0, j)),
                  pl.BlockSpec((D_INNER, tn), lambda j, i: (0, j)),
                  pl.BlockSpec((ROW_TILE, tn), lambda j, i: (i, j)),
                  pl.BlockSpec((ROW_TILE, tn), lambda j, i: (i, nt + j))],
        out_specs=pl.BlockSpec((ROW_TILE, tn), lambda j, i: (i, j)),
        out_shape=jax.ShapeDtypeStruct((N_ROWS, D_MODEL), BF16),
        scratch_shapes=[pltpu.VMEM((D_MODEL, tn), BF16), pltpu.VMEM((D_INNER, tn), BF16)],
        compiler_params=_params(2, 48),
        name="branch_merge",
    )(y_pool, y_ssm, w_bp, w_bs, gates, gates)


def _pool_prompt_kernel(u_ref, uprev_ref, pw_ref, ps_ref, o_ref, stage):
    lt = pl.program_id(0) % TILES_PER_SEQ
    halo = POOL_BUF + 1
    stage[0:halo, :] = jnp.where(lt == 0, 0.0, uprev_ref[...])
    stage[halo:halo + ROW_TILE, :] = u_ref[...]
    t = lt * ROW_TILE + lax.broadcasted_iota(jnp.int32, (ROW_TILE, 1), 0)
    for g, w in enumerate(POOL_WINDOWS):
        lo = g * POOL_GROUP
        cur = u_ref[:, lo:lo + POOL_GROUP]
        s = cur
        for j in range(1, w):
            s = s + stage[halo - j:halo - j + ROW_TILE, lo:lo + POOL_GROUP]
        cnt = jnp.minimum(w, t + 1).astype(F32)
        pooled = (s / cnt - cur).astype(BF16)
        mixed = jnp.dot(pooled, pw_ref[g].astype(BF16), preferred_element_type=F32)
        o_ref[:, lo:lo + POOL_GROUP] = (mixed * ps_ref[:, lo:lo + POOL_GROUP]).astype(BF16)


def _pool_prompt(u_pool, pool_w, pool_scale):
    halo = POOL_BUF + 1
    per = ROW_TILE // halo
    return pl.pallas_call(
        _pool_prompt_kernel,
        grid=(PROMPT_TILES,),
        in_specs=[pl.BlockSpec((ROW_TILE, D_MODEL), lambda i: (i, 0)),
                  pl.BlockSpec((halo, D_MODEL), lambda i: (jnp.maximum(i * per - 1, 0), 0)),
                  pl.BlockSpec((len(POOL_WINDOWS), POOL_GROUP, POOL_GROUP), lambda i: (0, 0, 0)),
                  pl.BlockSpec((1, D_MODEL), lambda i: (0, 0))],
        out_specs=pl.BlockSpec((ROW_TILE, D_MODEL), lambda i: (i, 0)),
        out_shape=jax.ShapeDtypeStruct((N_PROMPT, D_MODEL), BF16),
        scratch_shapes=[pltpu.VMEM((halo + ROW_TILE, D_MODEL), F32)],
        compiler_params=_params(1, 48),
        name="pool_prompt",
    )(u_pool, u_pool, pool_w, pool_scale)


def _pool_sample_kernel(u_ref, buf_ref, pw_ref, ps_ref, o_ref):
    n_prev = min(PAST_LEN, POOL_BUF)
    for g, w in enumerate(POOL_WINDOWS):
        lo = g * POOL_GROUP

        def row(k):
            if k < POOL_BUF:
                return buf_ref[k, :, lo:lo + POOL_GROUP]
            return u_ref[k - POOL_BUF, :, lo:lo + POOL_GROUP]

        pooled = []
        for t in range(DEC_SEQ):
            cur = row(POOL_BUF + t)
            s = cur
            for j in range(1, w):
                s = s + row(POOL_BUF + t - j)
            cnt = float(min(w, t + 1 + n_prev))
            pooled.append(s / cnt - cur)
        pooled = jnp.concatenate(pooled, axis=0).astype(BF16)
        mixed = jnp.dot(pooled, pw_ref[g].astype(BF16), preferred_element_type=F32)
        y = (mixed * ps_ref[:, lo:lo + POOL_GROUP]).astype(BF16)
        for t in range(DEC_SEQ):
            o_ref[t, :, lo:lo + POOL_GROUP] = y[t * SAMPLE_BB:(t + 1) * SAMPLE_BB]


def _pool_sample(u_s, buf_t, pool_w, pool_scale):
    return pl.pallas_call(
        _pool_sample_kernel,
        grid=(DEC_BATCH // SAMPLE_BB,),
        in_specs=[pl.BlockSpec((DEC_SEQ, SAMPLE_BB, D_MODEL), lambda i: (0, i, 0)),
                  pl.BlockSpec((POOL_BUF, SAMPLE_BB, D_MODEL), lambda i: (0, i, 0)),
                  pl.BlockSpec((len(POOL_WINDOWS), POOL_GROUP, POOL_GROUP), lambda i: (0, 0, 0)),
                  pl.BlockSpec((1, D_MODEL), lambda i: (0, 0))],
        out_specs=pl.BlockSpec((DEC_SEQ, SAMPLE_BB, D_MODEL), lambda i: (0, i, 0)),
        out_shape=jax.ShapeDtypeStruct((DEC_SEQ, DEC_BATCH, D_MODEL), BF16),
        compiler_params=_params(1, 40),
        name="pool_sample",
    )(u_s, buf_t, pool_w, pool_scale)


def _gated_group_norm(y, x, z, dskip_ref, norm_ref):
    y = (y + x * dskip_ref[...]) * _silu(z)
    y = y * lax.rsqrt(jnp.mean(y * y, axis=-1, keepdims=True) + EPS)
    return (y * norm_ref[...]).astype(BF16)


def _ssd_prompt_kernel(xh_ref, xh_prev_ref, bm_ref, bm_prev_ref, cm_ref, cm_prev_ref, z_ref, dt_ref,
                       dtb_ref, alog_ref, cwx_ref, cbx_ref, cwb_ref, cbb_ref, cwc_ref, cbc_ref,
                       dskip_ref, norm_ref, y_ref, hout_ref, h_s, stx, stb, stc):
    c = pl.program_id(2)
    first = c == 0
    halo = 8

    @pl.when(first)
    def _():
        h_s[...] = jnp.zeros_like(h_s)

    def conv_silu(cur_ref, prev_ref, w_ref, b_ref, stage):
        stage[0:halo, :] = jnp.where(first, 0.0, prev_ref[...])
        stage[halo:halo + CHUNK, :] = cur_ref[...]
        acc = b_ref[...] + stage[halo - 3:halo - 3 + CHUNK, :] * w_ref[0:1, :]
        acc = acc + stage[halo - 2:halo - 2 + CHUNK, :] * w_ref[1:2, :]
        acc = acc + stage[halo - 1:halo - 1 + CHUNK, :] * w_ref[2:3, :]
        acc = acc + cur_ref[...] * w_ref[3:4, :]
        return _silu(acc)

    x = conv_silu(xh_ref, xh_prev_ref, cwx_ref, cbx_ref, stx)
    bm = conv_silu(bm_ref, bm_prev_ref, cwb_ref, cbb_ref, stb)
    cm = conv_silu(cm_ref, cm_prev_ref, cwc_ref, cbc_ref, stc)

    dt_t = jax.nn.softplus(dt_ref[...] + dtb_ref[...])
    la_t = dt_t * (-jnp.exp(alog_ref[...]))
    lane_t = lax.broadcasted_iota(jnp.int32, la_t.shape, 1)
    s_t = la_t
    k = 1
    while k < CHUNK:
        s_t = s_t + jnp.where(lane_t >= k, pltpu.roll(s_t, k, axis=1), 0.0)
        k *= 2

    cm_b = cm.astype(BF16)
    cb = lax.dot_general(cm_b, bm.astype(BF16), (((1,), (1,)), ((), ())), preferred_element_type=F32)
    bm_t = bm.T
    tri = (lax.broadcasted_iota(jnp.int32, (CHUNK, CHUNK), 0)
           >= lax.broadcasted_iota(jnp.int32, (CHUNK, CHUNK), 1))
    lane = lax.broadcasted_iota(jnp.int32, (CHUNK, LANES), 1)
    lo_half = lane < HEAD_DIM

    w_heads, bt_heads, e_heads = [], [], []
    for j in range(HEADS_PER_GROUP):
        row_s = s_t[j:j + 1, :]
        rowb = jnp.broadcast_to(row_s, (CHUNK, CHUNK))
        colb = rowb.T
        decay = jnp.exp(jnp.where(tri, colb - rowb, -jnp.inf))
        dt_row = dt_t[j:j + 1, :]
        w_heads.append((cb * decay * dt_row).astype(BF16))
        tail = jnp.exp(row_s[:, CHUNK - 1:CHUNK] - row_s) * dt_row
        bt_heads.append((bm_t * tail).astype(BF16))
        e_heads.append(jnp.exp(colb))

    ys = []
    for i in range(PAIRS_PER_GROUP):
        j0, j1 = 2 * i, 2 * i + 1
        xp = x[:, i * LANES:(i + 1) * LANES]
        rhs = jnp.concatenate([jnp.where(lo_half, xp, 0.0), jnp.where(lo_half, 0.0, xp)], axis=0).astype(BF16)
        lhs = jnp.concatenate([jnp.concatenate([w_heads[j0], w_heads[j1]], axis=1),
                               jnp.concatenate([bt_heads[j0], bt_heads[j1]], axis=1)], axis=0)
        out = jnp.dot(lhs, rhs, preferred_element_type=F32)
        h_old = h_s[i]
        inter = jnp.dot(cm_b, h_old.astype(BF16), preferred_element_type=F32)
        e_sel = jnp.where(lo_half, e_heads[j0], e_heads[j1])
        ys.append(out[0:CHUNK] + inter * e_sel)
        h_s[i] = h_old * e_sel[CHUNK - 1:CHUNK, :] + out[CHUNK:2 * CHUNK]
    y = jnp.concatenate(ys, axis=1)

    y_ref[...] = _gated_group_norm(y, x, z_ref[...], dskip_ref, norm_ref)

    @pl.when(c == N_CHUNKS - 1)
    def _():
        for i in range(PAIRS_PER_GROUP):
            hout_ref[0, i] = h_s[i].T


def _ssd_prompt(xbc, z, dt_t, dt_bias, a_log, conv_w, conv_b, dskip_e, ssm_norm):
    halo = 8
    per = CHUNK // halo

    def cur(width, col0):
        return pl.BlockSpec((CHUNK, width), lambda b, g, c: (b * N_CHUNKS + c, col0 + g))

    def prev(width, col0):
        return pl.BlockSpec((halo, width),
                            lambda b, g, c: (jnp.maximum((b * N_CHUNKS + c) * per - 1, 0), col0 + g))

    def par(rows, width, col0):
        return pl.BlockSpec((rows, width), lambda b, g, c: (0, col0 + g))

    head_col = pl.BlockSpec((HEADS_PER_GROUP, 1), lambda b, g, c: (g, 0))
    y, h = pl.pallas_call(
        _ssd_prompt_kernel,
        grid=(BATCH, N_GROUPS, N_CHUNKS),
        in_specs=[cur(GROUP_CH, 0), prev(GROUP_CH, 0),
                  cur(D_STATE, XBC_B_COL), prev(D_STATE, XBC_B_COL),
                  cur(D_STATE, XBC_C_COL), prev(D_STATE, XBC_C_COL),
                  cur(GROUP_CH, 0),
                  pl.BlockSpec((HEADS_PER_GROUP, CHUNK), lambda b, g, c: (g, b * N_CHUNKS + c)),
                  head_col, head_col,
                  par(CONV_W, GROUP_CH, 0), par(1, GROUP_CH, 0),
                  par(CONV_W, D_STATE, XBC_B_COL), par(1, D_STATE, XBC_B_COL),
                  par(CONV_W, D_STATE, XBC_C_COL), par(1, D_STATE, XBC_C_COL),
                  par(1, GROUP_CH, 0), par(1, GROUP_CH, 0)],
        out_specs=[pl.BlockSpec((CHUNK, GROUP_CH), lambda b, g, c: (b * N_CHUNKS + c, g)),
                   pl.BlockSpec((1, PAIRS_PER_GROUP, 2 * HEAD_DIM, D_STATE), lambda b, g, c: (b, g, 0, 0))],
        out_shape=[jax.ShapeDtypeStruct((N_PROMPT, D_INNER), BF16),
                   jax.ShapeDtypeStruct((BATCH, N_HEADS // 2, 2 * HEAD_DIM, D_STATE), F32)],
        scratch_shapes=[pltpu.VMEM((PAIRS_PER_GROUP, D_STATE, LANES), F32),
                        pltpu.VMEM((halo + CHUNK, GROUP_CH), F32),
                        pltpu.VMEM((halo + CHUNK, D_STATE), F32),
                        pltpu.VMEM((halo + CHUNK, D_STATE), F32)],
        compiler_params=_params(3, 40),
        name="ssd_prompt",
    )(xbc, xbc, xbc, xbc, xbc, xbc, z, dt_t, dt_bias, a_log,
      conv_w, conv_b, conv_w, conv_b, conv_w, conv_b, dskip_e, ssm_norm)
    return y, h


def _split3(v):
    hi = v.astype(BF16)
    r1 = v - hi.astype(F32)
    mid = r1.astype(BF16)
    lo = (r1 - mid.astype(F32)).astype(BF16)
    return hi, mid, lo


def _ssd_sample_kernel(xh_ref, xst_ref, bm_ref, bst_ref, cm_ref, cst_ref, z_ref, dt_ref, dtb_ref, alog_ref,
                       cwx_ref, cbx_ref, cwb_ref, cbb_ref, cwc_ref, cbc_ref, dskip_ref, norm_ref, h0_ref,
                       y_ref, hout_ref):
    g = pl.program_id(1)
    bb = SAMPLE_BB
    rows = DEC_SEQ * bb

    def conv_silu(cur_ref, st_ref, w_ref, b_ref):
        full = [st_ref[k] for k in range(CONV_W - 1)] + [cur_ref[t] for t in range(DEC_SEQ)]
        outs = []
        for t in range(DEC_SEQ):
            acc = b_ref[...] + full[t] * w_ref[0:1, :]
            for k in range(1, CONV_W):
                acc = acc + full[t + k] * w_ref[k:k + 1, :]
            outs.append(_silu(acc))
        return outs

    x = conv_silu(xh_ref, xst_ref, cwx_ref, cbx_ref)
    bm = conv_silu(bm_ref, bst_ref, cwb_ref, cbb_ref)
    cm = conv_silu(cm_ref, cst_ref, cwc_ref, cbc_ref)

    a = -jnp.exp(alog_ref[...])
    dt = [jax.nn.softplus(dt_ref[t] + dtb_ref[...]) for t in range(DEC_SEQ)]
    s = [dt[0] * a]
    for t in range(1, DEC_SEQ):
        s.append(s[t - 1] + dt[t] * a)
    sel = (lax.broadcasted_iota(jnp.int32, (LANES, GROUP_CH), 0)
           == g * HEADS_PER_GROUP + lax.broadcasted_iota(jnp.int32, (LANES, GROUP_CH), 1) // HEAD_DIM)
    sel = jnp.where(sel, 1.0, 0.0).astype(BF16)

    def expand(v):
        return sum(jnp.dot(p, sel, preferred_element_type=F32) for p in _split3(v))

    dt_e = expand(jnp.concatenate(dt, axis=0))
    s_e = expand(jnp.concatenate(s, axis=0))
    dt_e = [dt_e[t * bb:(t + 1) * bb] for t in range(DEC_SEQ)]
    s_e = [s_e[t * bb:(t + 1) * bb] for t in range(DEC_SEQ)]

    c_all = jnp.concatenate(cm, axis=0).astype(BF16)
    b_all = jnp.concatenate(bm, axis=0)
    last = DEC_SEQ - 1
    xw_all = jnp.concatenate([x[t] * dt_e[t] * jnp.exp(s_e[last] - s_e[t]) for t in range(DEC_SEQ)], axis=0)
    xw_t = xw_all.T.astype(BF16)
    dec_t = jnp.concatenate([jnp.exp(s_e[last])] + [jnp.zeros((bb, GROUP_CH), F32)] * last, axis=0).T
    row_b = lax.broadcasted_iota(jnp.int32, (rows, 1), 0) % bb
    inter = jnp.zeros((rows, GROUP_CH), F32)
    for b in range(bb):
        mine = row_b == b
        h0 = h0_ref[b, 0]
        yb = lax.dot_general(c_all, h0.astype(BF16), (((1,), (1,)), ((), ())), preferred_element_type=F32)
        inter = inter + jnp.where(mine, yb, 0.0)
        dh = jnp.dot(xw_t, jnp.where(mine, b_all, 0.0).astype(BF16), preferred_element_type=F32)
        hout_ref[b, 0] = h0 * jnp.broadcast_to(dec_t[:, b:b + 1], (GROUP_CH, D_STATE)) + dh

    for t in range(DEC_SEQ):
        y = inter[t * bb:(t + 1) * bb] * jnp.exp(s_e[t])
        for u in range(t + 1):
            cb = jnp.sum(cm[t] * bm[u], axis=-1, keepdims=True)
            y = y + cb * jnp.exp(s_e[t] - s_e[u]) * dt_e[u] * x[u]
        y_ref[t] = _gated_group_norm(y, x[t], z_ref[t], dskip_ref, norm_ref)


def _ssd_sample(xbc_s, conv_st, z_s, dt_s, dt_bias_row, a_log_row, conv_w, conv_b, dskip_e, ssm_norm, h0):
    bb = SAMPLE_BB

    def cur(width, col0):
        return pl.BlockSpec((DEC_SEQ, bb, width), lambda i, g: (0, i, col0 + g))

    def st(width, col0):
        return pl.BlockSpec((CONV_W - 1, bb, width), lambda i, g: (0, i, col0 + g))

    def par(rows, width, col0):
        return pl.BlockSpec((rows, width), lambda i, g: (0, col0 + g))

    head_row = pl.BlockSpec((1, LANES), lambda i, g: (0, 0))
    state = pl.BlockSpec((bb, 1, GROUP_CH, D_STATE), lambda i, g: (i, g, 0, 0))
    y, h = pl.pallas_call(
        _ssd_sample_kernel,
        grid=(DEC_BATCH // bb, N_GROUPS),
        in_specs=[cur(GROUP_CH, 0), st(GROUP_CH, 0),
                  cur(D_STATE, XBC_B_COL), st(D_STATE, XBC_B_COL),
                  cur(D_STATE, XBC_C_COL), st(D_STATE, XBC_C_COL),
                  cur(GROUP_CH, 0),
                  pl.BlockSpec((DEC_SEQ, bb, LANES), lambda i, g: (0, i, 0)),
                  head_row, head_row,
                  par(CONV_W, GROUP_CH, 0), par(1, GROUP_CH, 0),
                  par(CONV_W, D_STATE, XBC_B_COL), par(1, D_STATE, XBC_B_COL),
                  par(CONV_W, D_STATE, XBC_C_COL), par(1, D_STATE, XBC_C_COL),
                  par(1, GROUP_CH, 0), par(1, GROUP_CH, 0),
                  state],
        out_specs=[pl.BlockSpec((DEC_SEQ, bb, GROUP_CH), lambda i, g: (0, i, g)), state],
        out_shape=[jax.ShapeDtypeStruct((DEC_SEQ, DEC_BATCH, D_INNER), BF16),
                   jax.ShapeDtypeStruct((DEC_BATCH, N_GROUPS, GROUP_CH, D_STATE), F32)],
        compiler_params=_params(2, 56),
        name="ssd_sample",
    )(xbc_s, conv_st, xbc_s, conv_st, xbc_s, conv_st, z_s, dt_s, dt_bias_row, a_log_row,
      conv_w, conv_b, conv_w, conv_b, conv_w, conv_b, dskip_e, ssm_norm, h0)
    return y, h


def kernel(x_prompt, x_sample, c_prompt, c_sample, state_ssm, state_conv, state_pool, w_ada, b_ada, norm_ffn1,
           w13_ffn1, w2_ffn1, norm_mix, w_in, pool_w, pool_scale, conv_w, conv_b, dt_bias, a_log, d_skip,
           ssm_norm, w_branch_pool, w_branch_ssm, w_out, norm_ffn2, w13_ffn2, w2_ffn2, norm_final):
    d = D_MODEL
    x_rows = jnp.concatenate([x_prompt.reshape(N_PROMPT, d),
                              x_sample.transpose(1, 0, 2).reshape(N_SAMPLE, d)], axis=0)

    n_c = BATCH + DEC_BATCH
    c_all = jnp.pad(jnp.concatenate([c_prompt, c_sample], axis=0), ((0, -n_c % 8), (0, 0)))
    mods = _ada(c_all, w_ada[0], b_ada[0].reshape(1, N_MOD * d))
    mods_p = mods[:BATCH]
    mods_s = jnp.tile(mods[BATCH:n_c], (DEC_SEQ, 1))

    a1 = _normmod(x_rows, norm_ffn1[0], mods_p, mods_s, 0, 1)
    h1 = _mm_res(_ffn_up(a1, w13_ffn1[0]), w2_ffn1[0], x_rows, mods_p, mods_s, 2, 0.5, 256)

    u = _normmod(h1, norm_mix[0], mods_p, mods_s, 3, 4)
    w_in0 = w_in[0]
    col_z, col_xbc, col_dt = d, d + D_INNER, d + D_INNER + CONV_DIM
    u_pool = _mm_plain(u, w_in0, 0, d, 512)
    z = _mm_plain(u, w_in0, col_z, D_INNER, 512)
    xbc = _mm_plain(u, w_in0, col_xbc, CONV_DIM, 512)
    gates = _mm_plain(u, w_in0[:, col_dt + N_HEADS:], 0, 2 * d, 512)
    dt_raw = _mm_plain(u, jnp.pad(w_in0[:, col_dt:col_dt + N_HEADS], ((0, 0), (0, LANES - N_HEADS))),
                       0, LANES, LANES)

    ps = pool_scale[0].reshape(1, d)
    u_pool_s = u_pool[N_PROMPT:].reshape(DEC_SEQ, DEC_BATCH, d)
    y_pool = jnp.concatenate([
        _pool_prompt(u_pool, pool_w[0], ps),
        _pool_sample(u_pool_s, state_pool[0].transpose(1, 0, 2), pool_w[0], ps).reshape(N_SAMPLE, d)], axis=0)

    conv_w0 = conv_w[0]
    conv_b0 = conv_b[0].reshape(1, CONV_DIM)
    dskip_e = jnp.repeat(d_skip[0], HEAD_DIM).reshape(1, D_INNER)
    norm_row = ssm_norm[0].reshape(1, D_INNER)
    y_ssm_p, h_p = _ssd_prompt(xbc, z, dt_raw[:N_PROMPT, :N_HEADS].T, dt_bias[0].reshape(N_HEADS, 1),
                               a_log[0].reshape(N_HEADS, 1), conv_w0, conv_b0, dskip_e, norm_row)
    xbc_s = xbc[N_PROMPT:].reshape(DEC_SEQ, DEC_BATCH, CONV_DIM)
    head_pad = (0, LANES - N_HEADS)
    y_ssm_s, h_s = _ssd_sample(
        xbc_s, state_conv[0].transpose(1, 0, 2), z[N_PROMPT:].reshape(DEC_SEQ, DEC_BATCH, D_INNER),
        dt_raw[N_PROMPT:].reshape(DEC_SEQ, DEC_BATCH, LANES),
        jnp.pad(dt_bias[0], head_pad).reshape(1, LANES), jnp.pad(a_log[0], head_pad).reshape(1, LANES),
        conv_w0, conv_b0, dskip_e, norm_row, state_ssm[0].reshape(DEC_BATCH, N_GROUPS, GROUP_CH, D_STATE))
    y_ssm = jnp.concatenate([y_ssm_p, y_ssm_s.reshape(N_SAMPLE, D_INNER)], axis=0)

    merged = _merge(y_pool, y_ssm, w_branch_pool[0], w_branch_ssm[0], gates)
    h2 = _mm_res(merged, w_out[0], h1, mods_p, mods_s, 5, 1.0, 512)
    a3 = _normmod(h2, norm_ffn2[0], mods_p, mods_s, 6, 7)
    h3 = _mm_res(_ffn_up(a3, w13_ffn2[0]), w2_ffn2[0], h2, mods_p, mods_s, 8, 0.5, 256)
    y = _final_norm(h3, norm_final)

    y_prompt = y[:N_PROMPT].reshape(BATCH, SEQ, d)
    y_sample = y[N_PROMPT:].reshape(DEC_SEQ, DEC_BATCH, d).transpose(1, 0, 2)
    xbc_p = xbc[:N_PROMPT].reshape(BATCH, SEQ, CONV_DIM)
    u_pool_p = u_pool[:N_PROMPT].reshape(BATCH, SEQ, d)
    keep = CONV_W - 1
    prompt_conv = xbc_p[:, SEQ - keep:][None]
    prompt_pool = u_pool_p[:, SEQ - POOL_BUF:][None]
    sample_conv = jnp.concatenate([state_conv[0], xbc_s.transpose(1, 0, 2)], axis=1)[:, -keep:][None]
    sample_pool = jnp.concatenate([state_pool[0], u_pool_s.transpose(1, 0, 2)], axis=1)[:, -POOL_BUF:][None]
    prompt_ssm = h_p.reshape(1, BATCH, N_HEADS, HEAD_DIM, D_STATE)
    sample_ssm = h_s.reshape(1, DEC_BATCH, N_HEADS, HEAD_DIM, D_STATE)
    return (y_prompt, y_sample, prompt_ssm, prompt_conv, prompt_pool, sample_ssm, sample_conv, sample_pool)
```

```python
import functools

import jax
import jax.numpy as jnp
from jax import lax
from jax.experimental import pallas as pl
from jax.experimental.pallas import tpu as pltpu

F32 = jnp.float32
BF16 = jnp.bfloat16

D_MODEL = 2048
BATCH = 4
SEQ = 2048
DEC_BATCH = 128
DEC_SEQ = 4
PAST_LEN = 16384
POOL_WINDOWS = (2, 4, 8, 16)
POOL_GROUP = D_MODEL // len(POOL_WINDOWS)
POOL_BUF = max(POOL_WINDOWS) - 1
D_INNER = 2 * D_MODEL
HEAD_DIM = 64
N_HEADS = D_INNER // HEAD_DIM
D_STATE = 128
N_GROUPS = 8
HEADS_PER_GROUP = N_HEADS // N_GROUPS
GROUP_CH = D_INNER // N_GROUPS
CONV_W = 4
CONV_DIM = D_INNER + 2 * N_GROUPS * D_STATE
CHUNK = 128
D_FF = 256 * ((8 * D_MODEL // 3 + 255) // 256)
N_MOD = 9
EPS = 1e-6

N_PROMPT = BATCH * SEQ
N_SAMPLE = DEC_BATCH * DEC_SEQ
N_ROWS = N_PROMPT + N_SAMPLE
ROW_TILE = 512
N_ROW_TILES = N_ROWS // ROW_TILE
N_CHUNKS = SEQ // CHUNK
LANES = 128
PAIRS_PER_GROUP = HEADS_PER_GROUP // 2
SAMPLE_BB = 32
XBC_B0 = D_INNER
XBC_C0 = D_INNER + N_GROUPS * D_STATE
XBC_B_COL = XBC_B0 // D_STATE
XBC_C_COL = XBC_C0 // D_STATE
CONV_HALO = 8
MIB = 1024 * 1024


def _params(n_axes, vmem_mib):
    return pltpu.CompilerParams(dimension_semantics=("arbitrary",) * n_axes,
                                vmem_limit_bytes=vmem_mib * MIB)


def _silu(x):
    return x * jax.nn.sigmoid(x)


def _row_mods(mp_ref, ms_ref, tile, tm):
    seq = jnp.minimum(tile // (SEQ // tm), BATCH - 1)
    ms = ms_ref[...]
    ms = jnp.concatenate([ms] * (tm // DEC_BATCH), axis=0)
    return jnp.where(tile >= N_PROMPT // tm, ms, mp_ref[pl.ds(seq, 1), :])


def _rows2(xp_ref, xs_ref, tile, tm):
    return jnp.where(tile >= N_PROMPT // tm, xs_ref[...], xp_ref[...])


def _spec_p(tm, width, col=0):
    last = N_PROMPT // tm - 1
    return pl.BlockSpec((tm, width), lambda i: (jnp.minimum(i, last), col))


def _spec_s(tm, width, col=0):
    first = N_PROMPT // tm
    return pl.BlockSpec((tm, width), lambda i: (jnp.maximum(i - first, 0), col))


def _mod_specs(chunk):
    return [pl.BlockSpec((BATCH, D_MODEL), lambda i: (0, chunk)),
            pl.BlockSpec((DEC_BATCH, D_MODEL), lambda i: (0, chunk))]


def _rms(x):
    return x * lax.rsqrt(jnp.mean(x * x, axis=-1, keepdims=True) + EPS)


def _ada_kernel(c_ref, w_ref, b_ref, o_ref):
    a = _silu(c_ref[...]).astype(BF16)
    o_ref[...] = jnp.dot(a, w_ref[...].astype(BF16), preferred_element_type=F32) + b_ref[...]


def _ada(c_all, w, b):
    m = c_all.shape[0]
    n = w.shape[1]
    tn = 1024
    return pl.pallas_call(
        _ada_kernel,
        grid=(n // tn,),
        in_specs=[pl.BlockSpec((m, D_MODEL), lambda j: (0, 0)),
                  pl.BlockSpec((D_MODEL, tn), lambda j: (0, j)),
                  pl.BlockSpec((1, tn), lambda j: (0, j))],
        out_specs=pl.BlockSpec((m, tn), lambda j: (0, j)),
        out_shape=jax.ShapeDtypeStruct((m, n), F32),
        compiler_params=_params(1, 40),
        name="ada_mods",
    )(c_all, w, b)


def _normmod_kernel(xp_ref, xs_ref, g_ref, shp_ref, shs_ref, scp_ref, scs_ref, o_ref):
    i = pl.program_id(0)
    y = _rms(_rows2(xp_ref, xs_ref, i, ROW_TILE)) * g_ref[...]
    shift = _row_mods(shp_ref, shs_ref, i, ROW_TILE)
    scale = _row_mods(scp_ref, scs_ref, i, ROW_TILE)
    o_ref[...] = (y * (1.0 + scale) + shift).astype(BF16)


def _normmod(xp, xs, g, mods_p, mods_s, shift_chunk, scale_chunk):
    return pl.pallas_call(
        _normmod_kernel,
        grid=(N_ROW_TILES,),
        in_specs=[_spec_p(ROW_TILE, D_MODEL), _spec_s(ROW_TILE, D_MODEL),
                  pl.BlockSpec((1, D_MODEL), lambda i: (0, 0))]
                 + _mod_specs(shift_chunk) + _mod_specs(scale_chunk),
        out_specs=pl.BlockSpec((ROW_TILE, D_MODEL), lambda i: (i, 0)),
        out_shape=jax.ShapeDtypeStruct((N_ROWS, D_MODEL), BF16),
        compiler_params=_params(1, 48),
        name="norm_modulate",
    )(xp, xs, g, mods_p, mods_s, mods_p, mods_s)


def _ffn_up_kernel(a_ref, wa_ref, wb_ref, o_ref, wa_s, wb_s):
    @pl.when(pl.program_id(1) == 0)
    def _():
        wa_s[...] = wa_ref[...].astype(BF16)
        wb_s[...] = wb_ref[...].astype(BF16)

    a = a_ref[...]
    ha = jnp.dot(a, wa_s[...], preferred_element_type=F32)
    hb = jnp.dot(a, wb_s[...], preferred_element_type=F32)
    o_ref[...] = (_silu(ha) * hb).astype(BF16)


def _ffn_up(a, w13):
    tn = 512
    nt = D_FF // tn
    return pl.pallas_call(
        _ffn_up_kernel,
        grid=(nt, N_ROW_TILES),
        in_specs=[pl.BlockSpec((ROW_TILE, D_MODEL), lambda j, i: (i, 0)),
                  pl.BlockSpec((D_MODEL, tn), lambda j, i: (0, j)),
                  pl.BlockSpec((D_MODEL, tn), lambda j, i: (0, nt + j))],
        out_specs=pl.BlockSpec((ROW_TILE, tn), lambda j, i: (i, j)),
        out_shape=jax.ShapeDtypeStruct((N_ROWS, D_FF), BF16),
        scratch_shapes=[pltpu.VMEM((D_MODEL, tn), BF16), pltpu.VMEM((D_MODEL, tn), BF16)],
        compiler_params=_params(2, 48),
        name="ffn_up",
    )(a, w13, w13)


def _mm_plain_kernel(a_ref, w_ref, o_ref, w_s):
    @pl.when(pl.program_id(1) == 0)
    def _():
        w_s[...] = w_ref[...].astype(BF16)

    o_ref[...] = jnp.dot(a_ref[...], w_s[...], preferred_element_type=F32).astype(o_ref.dtype)


def _mm_plain(a, w, col0, ncols, tn, out_dtype):
    k = a.shape[1]
    nt = ncols // tn
    off = col0 // tn
    return pl.pallas_call(
        _mm_plain_kernel,
        grid=(nt, N_ROW_TILES),
        in_specs=[pl.BlockSpec((ROW_TILE, k), lambda j, i: (i, 0)),
                  pl.BlockSpec((k, tn), lambda j, i: (0, off + j))],
        out_specs=pl.BlockSpec((ROW_TILE, tn), lambda j, i: (i, j)),
        out_shape=jax.ShapeDtypeStruct((N_ROWS, ncols), out_dtype),
        scratch_shapes=[pltpu.VMEM((k, tn), BF16)],
        compiler_params=_params(2, 48),
        name="mm_in_proj",
    )(a, w)


def _load_weight(w_hbm, w_s, stage, sem, chunk):
    n = w_hbm.shape[0] // chunk

    def copy(c):
        return pltpu.make_async_copy(w_hbm.at[pl.ds(c * chunk, chunk), :], stage.at[c % 2], sem.at[c % 2])

    copy(0).start()
    for c in range(n):
        if c + 1 < n:
            copy(c + 1).start()
        copy(c).wait()
        w_s[c * chunk:(c + 1) * chunk, :] = stage[c % 2].astype(BF16)


def _res_block_kernel(*refs, tm, scale, two_source_res, final, chunk):
    refs = list(refs)
    a_ref, w_hbm = refs[:2]
    pos = 2
    if two_source_res:
        resp_ref, ress_ref = refs[pos:pos + 2]
        pos += 2
    else:
        res_ref = refs[pos]
        pos += 1
    gp_ref, gs_ref, g_ref = refs[pos:pos + 3]
    pos += 3
    if not final:
        shp_ref, shs_ref, scp_ref, scs_ref = refs[pos:pos + 4]
        pos += 4
    outs = refs[pos:pos + 2]
    w_s, stage, sem = refs[pos + 2:]
    i = pl.program_id(0)

    @pl.when(i == 0)
    def _():
        _load_weight(w_hbm, w_s, stage, sem, chunk)

    res = _rows2(resp_ref, ress_ref, i, tm) if two_source_res else res_ref[...]
    gate = _row_mods(gp_ref, gs_ref, i, tm)
    if scale != 1.0:
        gate = scale * gate
    h = res + gate * jnp.dot(a_ref[...], w_s[...], preferred_element_type=F32)
    y = _rms(h) * g_ref[...]
    if final:
        yp_ref, ys_ref = outs

        @pl.when(i < N_PROMPT // tm)
        def _():
            yp_ref[...] = y

        @pl.when(i >= N_PROMPT // tm)
        def _():
            ys_ref[...] = y
    else:
        h_ref, nxt_ref = outs
        h_ref[...] = h
        shift = _row_mods(shp_ref, shs_ref, i, tm)
        sc = _row_mods(scp_ref, scs_ref, i, tm)
        nxt_ref[...] = (y * (1.0 + sc) + shift).astype(BF16)


def _res_block(a, w, res, mods_p, mods_s, gate_chunk, scale, norm_g, next_chunks, tm, chunk, vmem_mib):
    k = a.shape[1]
    final = next_chunks is None
    two = isinstance(res, tuple)
    row = pl.BlockSpec((tm, D_MODEL), lambda i: (i, 0))
    in_specs = [pl.BlockSpec((tm, k), lambda i: (i, 0)), pl.BlockSpec(memory_space=pl.ANY)]
    args = [a, w]
    if two:
        in_specs += [_spec_p(tm, D_MODEL), _spec_s(tm, D_MODEL)]
        args += list(res)
    else:
        in_specs.append(row)
        args.append(res)
    in_specs += _mod_specs(gate_chunk) + [pl.BlockSpec((1, D_MODEL), lambda i: (0, 0))]
    args += [mods_p, mods_s, norm_g]
    if final:
        out_specs = [_spec_p(tm, D_MODEL), _spec_s(tm, D_MODEL)]
        out_shape = [jax.ShapeDtypeStruct((N_PROMPT, D_MODEL), F32), jax.ShapeDtypeStruct((N_SAMPLE, D_MODEL), F32)]
    else:
        in_specs += _mod_specs(next_chunks[0]) + _mod_specs(next_chunks[1])
        args += [mods_p, mods_s, mods_p, mods_s]
        out_specs = [row, row]
        out_shape = [jax.ShapeDtypeStruct((N_ROWS, D_MODEL), F32), jax.ShapeDtypeStruct((N_ROWS, D_MODEL), BF16)]
    return pl.pallas_call(
        functools.partial(_res_block_kernel, tm=tm, scale=scale, two_source_res=two, final=final, chunk=chunk),
        grid=(N_ROWS // tm,),
        in_specs=in_specs,
        out_specs=out_specs,
        out_shape=out_shape,
        scratch_shapes=[pltpu.VMEM((k, D_MODEL), BF16), pltpu.VMEM((2, chunk, D_MODEL), F32),
                        pltpu.SemaphoreType.DMA((2,))],
        compiler_params=_params(1, vmem_mib),
        name="res_block_final" if final else "res_block",
    )(*args)


def _merge_kernel(ypp_ref, yps_ref, ysp_ref, yss_ref, gp_ref, gs_ref, wp_hbm, ws_hbm, o_ref,
                  wp_s, ws_s, stage, sem, *, tm, chunk):
    i = pl.program_id(0)

    @pl.when(i == 0)
    def _():
        _load_weight(wp_hbm, wp_s, stage, sem, chunk)
        _load_weight(ws_hbm, ws_s, stage, sem, chunk)

    mp = jnp.dot(_rows2(ypp_ref, yps_ref, i, tm), wp_s[...], preferred_element_type=F32)
    ms = jnp.dot(_rows2(ysp_ref, yss_ref, i, tm), ws_s[...], preferred_element_type=F32)
    o_ref[...] = (jax.nn.sigmoid(gp_ref[...].astype(F32)) * mp
                  + jax.nn.sigmoid(gs_ref[...].astype(F32)) * ms).astype(BF16)


def _merge(yp_p, yp_s, ys_p, ys_s, gates, w_bp, w_bs):
    tm, chunk = 256, 256
    return pl.pallas_call(
        functools.partial(_merge_kernel, tm=tm, chunk=chunk),
        grid=(N_ROWS // tm,),
        in_specs=[_spec_p(tm, D_MODEL), _spec_s(tm, D_MODEL), _spec_p(tm, D_INNER), _spec_s(tm, D_INNER),
                  pl.BlockSpec((tm, D_MODEL), lambda i: (i, 0)), pl.BlockSpec((tm, D_MODEL), lambda i: (i, 1)),
                  pl.BlockSpec(memory_space=pl.ANY), pl.BlockSpec(memory_space=pl.ANY)],
        out_specs=pl.BlockSpec((tm, D_MODEL), lambda i: (i, 0)),
        out_shape=jax.ShapeDtypeStruct((N_ROWS, D_MODEL), BF16),
        scratch_shapes=[pltpu.VMEM((D_MODEL, D_MODEL), BF16), pltpu.VMEM((D_INNER, D_MODEL), BF16),
                        pltpu.VMEM((2, chunk, D_MODEL), F32), pltpu.SemaphoreType.DMA((2,))],
        compiler_params=_params(1, 52),
        name="branch_merge",
    )(yp_p, yp_s, ys_p, ys_s, gates, gates, w_bp, w_bs)


def _pool_prompt_kernel(u_ref, uprev_ref, pw_ref, ps_ref, o_ref, stage):
    tiles_per_seq = SEQ // ROW_TILE
    lt = pl.program_id(0) % tiles_per_seq
    halo = POOL_BUF + 1
    stage[0:halo, :] = jnp.where(lt == 0, 0.0, uprev_ref[...])
    stage[halo:halo + ROW_TILE, :] = u_ref[...]
    t = lt * ROW_TILE + lax.broadcasted_iota(jnp.int32, (ROW_TILE, 1), 0)
    for g, w in enumerate(POOL_WINDOWS):
        lo = g * POOL_GROUP
        cur = u_ref[:, lo:lo + POOL_GROUP]
        s = cur
        for j in range(1, w):
            s = s + stage[halo - j:halo - j + ROW_TILE, lo:lo + POOL_GROUP]
        cnt = jnp.minimum(w, t + 1).astype(F32)
        pooled = (s / cnt - cur).astype(BF16)
        mixed = jnp.dot(pooled, pw_ref[g].astype(BF16), preferred_element_type=F32)
        o_ref[:, lo:lo + POOL_GROUP] = (mixed * ps_ref[:, lo:lo + POOL_GROUP]).astype(BF16)


def _pool_prompt(u_pool, pool_w, pool_scale):
    halo = POOL_BUF + 1
    per = ROW_TILE // halo
    return pl.pallas_call(
        _pool_prompt_kernel,
        grid=(N_PROMPT // ROW_TILE,),
        in_specs=[pl.BlockSpec((ROW_TILE, D_MODEL), lambda i: (i, 0)),
                  pl.BlockSpec((halo, D_MODEL), lambda i: (jnp.maximum(i * per - 1, 0), 0)),
                  pl.BlockSpec((len(POOL_WINDOWS), POOL_GROUP, POOL_GROUP), lambda i: (0, 0, 0)),
                  pl.BlockSpec((1, D_MODEL), lambda i: (0, 0))],
        out_specs=pl.BlockSpec((ROW_TILE, D_MODEL), lambda i: (i, 0)),
        out_shape=jax.ShapeDtypeStruct((N_PROMPT, D_MODEL), BF16),
        scratch_shapes=[pltpu.VMEM((halo + ROW_TILE, D_MODEL), F32)],
        compiler_params=_params(1, 48),
        name="pool_prompt",
    )(u_pool, u_pool, pool_w, pool_scale)


def _pool_sample_kernel(u_ref, buf_ref, pw_ref, ps_ref, o_ref):
    n_prev = min(PAST_LEN, POOL_BUF)
    for g, w in enumerate(POOL_WINDOWS):
        lo = g * POOL_GROUP

        def row(k):
            if k < POOL_BUF:
                return buf_ref[k, :, lo:lo + POOL_GROUP]
            return u_ref[k - POOL_BUF, :, lo:lo + POOL_GROUP]

        pooled = []
        for t in range(DEC_SEQ):
            cur = row(POOL_BUF + t)
            s = cur
            for j in range(1, w):
                s = s + row(POOL_BUF + t - j)
            cnt = float(min(w, t + 1 + n_prev))
            pooled.append(s / cnt - cur)
        pooled = jnp.concatenate(pooled, axis=0).astype(BF16)
        mixed = jnp.dot(pooled, pw_ref[g].astype(BF16), preferred_element_type=F32)
        y = (mixed * ps_ref[:, lo:lo + POOL_GROUP]).astype(BF16)
        for t in range(DEC_SEQ):
            o_ref[t, :, lo:lo + POOL_GROUP] = y[t * SAMPLE_BB:(t + 1) * SAMPLE_BB]


def _pool_sample(u_s, buf_t, pool_w, pool_scale):
    return pl.pallas_call(
        _pool_sample_kernel,
        grid=(DEC_BATCH // SAMPLE_BB,),
        in_specs=[pl.BlockSpec((DEC_SEQ, SAMPLE_BB, D_MODEL), lambda i: (0, i, 0)),
                  pl.BlockSpec((POOL_BUF, SAMPLE_BB, D_MODEL), lambda i: (0, i, 0)),
                  pl.BlockSpec((len(POOL_WINDOWS), POOL_GROUP, POOL_GROUP), lambda i: (0, 0, 0)),
                  pl.BlockSpec((1, D_MODEL), lambda i: (0, 0))],
        out_specs=pl.BlockSpec((DEC_SEQ, SAMPLE_BB, D_MODEL), lambda i: (0, i, 0)),
        out_shape=jax.ShapeDtypeStruct((DEC_SEQ, DEC_BATCH, D_MODEL), BF16),
        compiler_params=_params(1, 40),
        name="pool_sample",
    )(u_s, buf_t, pool_w, pool_scale)


def _gated_group_norm(y, x, z, dskip, norm):
    y = (y + x * dskip) * _silu(z.astype(F32))
    return (_rms(y) * norm).astype(BF16)


def _ssd_prompt_kernel(xbc_ref, prev_ref, z_ref, dt_ref, dtb_ref, alog_ref, cw_ref, cb_ref, dskip_ref, norm_ref,
                       y_ref, hout_ref, h_s, stage, act_s):
    c = pl.program_id(1)
    first = c == 0
    halo = CONV_HALO

    @pl.when(first)
    def _():
        h_s[...] = jnp.zeros_like(h_s)

    stage[0:halo, :] = jnp.where(first, 0.0, prev_ref[...])
    stage[halo:halo + CHUNK, :] = xbc_ref[...]
    slab = 512
    for lo in range(0, CONV_DIM, slab):
        cols = slice(lo, lo + slab)
        acc = cb_ref[:, cols] + stage[halo - 3:halo - 3 + CHUNK, cols] * cw_ref[0:1, cols]
        acc = acc + stage[halo - 2:halo - 2 + CHUNK, cols] * cw_ref[1:2, cols]
        acc = acc + stage[halo - 1:halo - 1 + CHUNK, cols] * cw_ref[2:3, cols]
        acc = acc + stage[halo:halo + CHUNK, cols] * cw_ref[3:4, cols]
        act_s[:, cols] = _silu(acc)

    dt_t = jax.nn.softplus(dt_ref[...] + dtb_ref[...])
    la_t = dt_t * (-jnp.exp(alog_ref[...]))
    lane_t = lax.broadcasted_iota(jnp.int32, la_t.shape, 1)
    s_t = la_t
    k = 1
    while k < CHUNK:
        s_t = s_t + jnp.where(lane_t >= k, pltpu.roll(s_t, k, axis=1), 0.0)
        k *= 2

    tri = (lax.broadcasted_iota(jnp.int32, (CHUNK, CHUNK), 0)
           >= lax.broadcasted_iota(jnp.int32, (CHUNK, CHUNK), 1))
    lo_half = lax.broadcasted_iota(jnp.int32, (CHUNK, LANES), 1) < HEAD_DIM

    for g in range(N_GROUPS):
        x = act_s[:, g * GROUP_CH:(g + 1) * GROUP_CH]
        bm = act_s[:, XBC_B0 + g * D_STATE:XBC_B0 + (g + 1) * D_STATE]
        cm_b = act_s[:, XBC_C0 + g * D_STATE:XBC_C0 + (g + 1) * D_STATE].astype(BF16)
        cb = lax.dot_general(cm_b, bm.astype(BF16), (((1,), (1,)), ((), ())), preferred_element_type=F32)
        bm_t = bm.T
        h_old = h_s[g]
        inter = jnp.dot(cm_b, h_old.astype(BF16), preferred_element_type=F32)

        ys, hs = [], []
        for i in range(PAIRS_PER_GROUP):
            w_pair, bt_pair, e_pair = [], [], []
            for j in (g * HEADS_PER_GROUP + 2 * i, g * HEADS_PER_GROUP + 2 * i + 1):
                row_s = s_t[j:j + 1, :]
                rowb = jnp.broadcast_to(row_s, (CHUNK, CHUNK))
                colb = rowb.T
                decay = jnp.exp(jnp.where(tri, colb - rowb, -jnp.inf))
                dt_row = dt_t[j:j + 1, :]
                w_pair.append((cb * decay * dt_row).astype(BF16))
                tail = jnp.exp(row_s[:, CHUNK - 1:CHUNK] - row_s) * dt_row
                bt_pair.append((bm_t * tail).astype(BF16))
                e_pair.append(jnp.exp(colb))
            xp = x[:, i * LANES:(i + 1) * LANES]
            rhs = jnp.concatenate([jnp.where(lo_half, xp, 0.0), jnp.where(lo_half, 0.0, xp)], axis=0).astype(BF16)
            lhs = jnp.concatenate([jnp.concatenate(w_pair, axis=1), jnp.concatenate(bt_pair, axis=1)], axis=0)
            out = jnp.dot(lhs, rhs, preferred_element_type=F32)
            e_sel = jnp.where(lo_half, e_pair[0], e_pair[1])
            ys.append(out[0:CHUNK] + inter[:, i * LANES:(i + 1) * LANES] * e_sel)
            hs.append(h_old[:, i * LANES:(i + 1) * LANES] * e_sel[CHUNK - 1:CHUNK, :] + out[CHUNK:2 * CHUNK])
        h_s[g] = jnp.concatenate(hs, axis=1)
        cols = slice(g * GROUP_CH, (g + 1) * GROUP_CH)
        y_ref[:, cols] = _gated_group_norm(jnp.concatenate(ys, axis=1), x, z_ref[:, cols],
                                           dskip_ref[:, cols], norm_ref[:, cols])

    @pl.when(c == N_CHUNKS - 1)
    def _():
        for g in range(N_GROUPS):
            for i in range(PAIRS_PER_GROUP):
                hout_ref[0, g * PAIRS_PER_GROUP + i] = h_s[g, :, i * LANES:(i + 1) * LANES].T


def _ssd_prompt(xbc, z, dt_t, dt_bias, a_log, conv_w, conv_b, dskip_e, ssm_norm):
    per = CHUNK // CONV_HALO

    def full(rows, width):
        return pl.BlockSpec((rows, width), lambda b, c: (0, 0))

    y, h = pl.pallas_call(
        _ssd_prompt_kernel,
        grid=(BATCH, N_CHUNKS),
        in_specs=[pl.BlockSpec((CHUNK, CONV_DIM), lambda b, c: (b * N_CHUNKS + c, 0)),
                  pl.BlockSpec((CONV_HALO, CONV_DIM),
                               lambda b, c: (jnp.maximum((b * N_CHUNKS + c) * per - 1, 0), 0)),
                  pl.BlockSpec((CHUNK, D_INNER), lambda b, c: (b * N_CHUNKS + c, 0)),
                  pl.BlockSpec((N_HEADS, CHUNK), lambda b, c: (0, b * N_CHUNKS + c)),
                  full(N_HEADS, 1), full(N_HEADS, 1),
                  full(CONV_W, CONV_DIM), full(1, CONV_DIM), full(1, D_INNER), full(1, D_INNER)],
        out_specs=[pl.BlockSpec((CHUNK, D_INNER), lambda b, c: (b * N_CHUNKS + c, 0)),
                   pl.BlockSpec((1, N_HEADS // 2, 2 * HEAD_DIM, D_STATE), lambda b, c: (b, 0, 0, 0))],
        out_shape=[jax.ShapeDtypeStruct((N_PROMPT, D_INNER), BF16),
                   jax.ShapeDtypeStruct((BATCH, N_HEADS // 2, 2 * HEAD_DIM, D_STATE), F32)],
        scratch_shapes=[pltpu.VMEM((N_GROUPS, D_STATE, GROUP_CH), F32),
                        pltpu.VMEM((CONV_HALO + CHUNK, CONV_DIM), F32),
                        pltpu.VMEM((CHUNK, CONV_DIM), F32)],
        compiler_params=_params(2, 48),
        name="ssd_prompt",
    )(xbc, xbc, z, dt_t, dt_bias, a_log, conv_w, conv_b, dskip_e, ssm_norm)
    return y, h


def _split3(v):
    hi = v.astype(BF16)
    r1 = v - hi.astype(F32)
    mid = r1.astype(BF16)
    lo = (r1 - mid.astype(F32)).astype(BF16)
    return hi, mid, lo


def _ssd_sample_kernel(xh_ref, xst_ref, bm_ref, bst_ref, cm_ref, cst_ref, z_ref, dt_ref, dtb_ref, alog_ref,
                       cwx_ref, cbx_ref, cwb_ref, cbb_ref, cwc_ref, cbc_ref, dskip_ref, norm_ref, h0_ref,
                       y_ref, hout_ref):
    g = pl.program_id(1)
    bb = SAMPLE_BB
    rows = DEC_SEQ * bb

    def conv_silu(cur_ref, st_ref, w_ref, b_ref):
        full = [st_ref[k] for k in range(CONV_W - 1)] + [cur_ref[t] for t in range(DEC_SEQ)]
        outs = []
        for t in range(DEC_SEQ):
            acc = b_ref[...] + full[t] * w_ref[0:1, :]
            for k in range(1, CONV_W):
                acc = acc + full[t + k] * w_ref[k:k + 1, :]
            outs.append(_silu(acc))
        return outs

    x = conv_silu(xh_ref, xst_ref, cwx_ref, cbx_ref)
    bm = conv_silu(bm_ref, bst_ref, cwb_ref, cbb_ref)
    cm = conv_silu(cm_ref, cst_ref, cwc_ref, cbc_ref)

    a = -jnp.exp(alog_ref[...])
    dt = [jax.nn.softplus(dt_ref[t] + dtb_ref[...]) for t in range(DEC_SEQ)]
    s = [dt[0] * a]
    for t in range(1, DEC_SEQ):
        s.append(s[t - 1] + dt[t] * a)
    sel = (lax.broadcasted_iota(jnp.int32, (LANES, GROUP_CH), 0)
           == g * HEADS_PER_GROUP + lax.broadcasted_iota(jnp.int32, (LANES, GROUP_CH), 1) // HEAD_DIM)
    sel = jnp.where(sel, 1.0, 0.0).astype(BF16)

    def expand(v):
        return sum(jnp.dot(p, sel, preferred_element_type=F32) for p in _split3(v))

    dt_e = expand(jnp.concatenate(dt, axis=0))
    s_e = expand(jnp.concatenate(s, axis=0))
    dt_e = [dt_e[t * bb:(t + 1) * bb] for t in range(DEC_SEQ)]
    s_e = [s_e[t * bb:(t + 1) * bb] for t in range(DEC_SEQ)]

    c_all = jnp.concatenate(cm, axis=0).astype(BF16)
    b_all = jnp.concatenate(bm, axis=0)
    last = DEC_SEQ - 1
    xw_all = jnp.concatenate([x[t] * dt_e[t] * jnp.exp(s_e[last] - s_e[t]) for t in range(DEC_SEQ)], axis=0)
    xw_t = xw_all.T.astype(BF16)
    dec_t = jnp.concatenate([jnp.exp(s_e[last])] + [jnp.zeros((bb, GROUP_CH), F32)] * last, axis=0).T
    row_b = lax.broadcasted_iota(jnp.int32, (rows, 1), 0) % bb
    inter = jnp.zeros((rows, GROUP_CH), F32)
    for b in range(bb):
        mine = row_b == b
        h0 = h0_ref[b, 0]
        yb = lax.dot_general(c_all, h0.astype(BF16), (((1,), (1,)), ((), ())), preferred_element_type=F32)
        inter = inter + jnp.where(mine, yb, 0.0)
        dh = jnp.dot(xw_t, jnp.where(mine, b_all, 0.0).astype(BF16), preferred_element_type=F32)
        hout_ref[b, 0] = h0 * jnp.broadcast_to(dec_t[:, b:b + 1], (GROUP_CH, D_STATE)) + dh

    for t in range(DEC_SEQ):
        y = inter[t * bb:(t + 1) * bb] * jnp.exp(s_e[t])
        for u in range(t + 1):
            cb = jnp.sum(cm[t] * bm[u], axis=-1, keepdims=True)
            y = y + cb * jnp.exp(s_e[t] - s_e[u]) * dt_e[u] * x[u]
        y_ref[t] = _gated_group_norm(y, x[t], z_ref[t], dskip_ref[...], norm_ref[...])


def _ssd_sample(xbc_s, conv_st, z_s, dt_s, dt_bias_row, a_log_row, conv_w, conv_b, dskip_e, ssm_norm, h0):
    bb = SAMPLE_BB

    def cur(width, col0):
        return pl.BlockSpec((DEC_SEQ, bb, width), lambda i, g: (0, i, col0 + g))

    def st(width, col0):
        return pl.BlockSpec((CONV_W - 1, bb, width), lambda i, g: (0, i, col0 + g))

    def par(rows, width, col0):
        return pl.BlockSpec((rows, width), lambda i, g: (0, col0 + g))

    head_row = pl.BlockSpec((1, LANES), lambda i, g: (0, 0))
    state = pl.BlockSpec((bb, 1, GROUP_CH, D_STATE), lambda i, g: (i, g, 0, 0))
    y, h = pl.pallas_call(
        _ssd_sample_kernel,
        grid=(DEC_BATCH // bb, N_GROUPS),
        in_specs=[cur(GROUP_CH, 0), st(GROUP_CH, 0),
                  cur(D_STATE, XBC_B_COL), st(D_STATE, XBC_B_COL),
                  cur(D_STATE, XBC_C_COL), st(D_STATE, XBC_C_COL),
                  cur(GROUP_CH, 0),
                  pl.BlockSpec((DEC_SEQ, bb, LANES), lambda i, g: (0, i, 0)),
                  head_row, head_row,
                  par(CONV_W, GROUP_CH, 0), par(1, GROUP_CH, 0),
                  par(CONV_W, D_STATE, XBC_B_COL), par(1, D_STATE, XBC_B_COL),
                  par(CONV_W, D_STATE, XBC_C_COL), par(1, D_STATE, XBC_C_COL),
                  par(1, GROUP_CH, 0), par(1, GROUP_CH, 0),
                  state],
        out_specs=[pl.BlockSpec((DEC_SEQ, bb, GROUP_CH), lambda i, g: (0, i, g)), state],
        out_shape=[jax.ShapeDtypeStruct((DEC_SEQ, DEC_BATCH, D_INNER), BF16),
                   jax.ShapeDtypeStruct((DEC_BATCH, N_GROUPS, GROUP_CH, D_STATE), F32)],
        compiler_params=_params(2, 56),
        name="ssd_sample",
    )(xbc_s, conv_st, xbc_s, conv_st, xbc_s, conv_st, z_s, dt_s, dt_bias_row, a_log_row,
      conv_w, conv_b, conv_w, conv_b, conv_w, conv_b, dskip_e, ssm_norm, h0)
    return y, h


def kernel(x_prompt, x_sample, c_prompt, c_sample, state_ssm, state_conv, state_pool, w_ada, b_ada, norm_ffn1,
           w13_ffn1, w2_ffn1, norm_mix, w_in, pool_w, pool_scale, conv_w, conv_b, dt_bias, a_log, d_skip,
           ssm_norm, w_branch_pool, w_branch_ssm, w_out, norm_ffn2, w13_ffn2, w2_ffn2, norm_final):
    d = D_MODEL

    def layer0(w):
        return w.reshape(w.shape[1:])

    def row(v):
        return v.reshape(1, -1)

    xp = x_prompt.reshape(N_PROMPT, d)
    xs = x_sample.transpose(1, 0, 2).reshape(N_SAMPLE, d)

    n_c = BATCH + DEC_BATCH
    c_all = jnp.pad(jnp.concatenate([c_prompt, c_sample], axis=0), ((0, -n_c % 8), (0, 0)))
    mods = _ada(c_all, layer0(w_ada), b_ada.reshape(1, N_MOD * d))
    mods_p = mods[:BATCH]
    mods_s = mods[BATCH:n_c]

    a1 = _normmod(xp, xs, row(norm_ffn1), mods_p, mods_s, 0, 1)
    act1 = _ffn_up(a1, layer0(w13_ffn1))
    h1, u = _res_block(act1, layer0(w2_ffn1), (xp, xs), mods_p, mods_s, 2, 0.5, row(norm_mix), (3, 4),
                       tm=256, chunk=176, vmem_mib=58)

    w_in0 = layer0(w_in)
    col_z, col_xbc, col_dt = d, d + D_INNER, d + D_INNER + CONV_DIM
    u_pool = _mm_plain(u, w_in0, 0, d, 1024, F32)
    z = _mm_plain(u, w_in0, col_z, D_INNER, 1024, BF16)
    xbc = _mm_plain(u, w_in0, col_xbc, CONV_DIM, 1024, F32)
    gates = _mm_plain(u, w_in0[:, col_dt + N_HEADS:], 0, 2 * d, 1024, BF16)
    dt_raw = _mm_plain(u, jnp.pad(w_in0[:, col_dt:col_dt + N_HEADS], ((0, 0), (0, LANES - N_HEADS))),
                       0, LANES, LANES, F32)

    pw = layer0(pool_w)
    ps = row(pool_scale)
    u_pool_s = u_pool[N_PROMPT:].reshape(DEC_SEQ, DEC_BATCH, d)
    pool_buf = layer0(state_pool)
    y_pool_p = _pool_prompt(u_pool, pw, ps)
    y_pool_s = _pool_sample(u_pool_s, pool_buf.transpose(1, 0, 2), pw, ps).reshape(N_SAMPLE, d)

    conv_w0 = layer0(conv_w)
    conv_b0 = row(conv_b)
    conv_buf = layer0(state_conv)
    dskip_e = row(jnp.repeat(d_skip.reshape(N_HEADS), HEAD_DIM))
    norm_row = row(ssm_norm)
    y_ssm_p, h_p = _ssd_prompt(xbc, z, dt_raw[:N_PROMPT, :N_HEADS].T, dt_bias.reshape(N_HEADS, 1),
                               a_log.reshape(N_HEADS, 1), conv_w0, conv_b0, dskip_e, norm_row)
    xbc_s = xbc[N_PROMPT:].reshape(DEC_SEQ, DEC_BATCH, CONV_DIM)
    head_pad = ((0, 0), (0, LANES - N_HEADS))
    y_ssm_s, h_s = _ssd_sample(
        xbc_s, conv_buf.transpose(1, 0, 2), z[N_PROMPT:].reshape(DEC_SEQ, DEC_BATCH, D_INNER),
        dt_raw[N_PROMPT:].reshape(DEC_SEQ, DEC_BATCH, LANES),
        jnp.pad(row(dt_bias), head_pad), jnp.pad(row(a_log), head_pad),
        conv_w0, conv_b0, dskip_e, norm_row, state_ssm.reshape(DEC_BATCH, N_GROUPS, GROUP_CH, D_STATE))

    merged = _merge(y_pool_p, y_pool_s, y_ssm_p, y_ssm_s.reshape(N_SAMPLE, D_INNER), gates,
                    layer0(w_branch_pool), layer0(w_branch_ssm))
    h2, a3 = _res_block(merged, layer0(w_out), h1, mods_p, mods_s, 5, 1.0, row(norm_ffn2), (6, 7),
                        tm=512, chunk=256, vmem_mib=52)
    act2 = _ffn_up(a3, layer0(w13_ffn2))
    y_p, y_s = _res_block(act2, layer0(w2_ffn2), h2, mods_p, mods_s, 8, 0.5, row(norm_final), None,
                          tm=256, chunk=176, vmem_mib=56)

    y_prompt = y_p.reshape(BATCH, SEQ, d)
    y_sample = y_s.reshape(DEC_SEQ, DEC_BATCH, d).transpose(1, 0, 2)
    keep = CONV_W - 1
    seq_end = (jnp.arange(BATCH) + 1) * SEQ
    conv_rows = (seq_end[:, None] - keep + jnp.arange(keep)[None, :]).reshape(-1)
    pool_rows = (seq_end[:, None] - POOL_BUF + jnp.arange(POOL_BUF)[None, :]).reshape(-1)
    prompt_conv = jnp.take(xbc, conv_rows, axis=0).reshape(1, BATCH, keep, CONV_DIM)
    prompt_pool = jnp.take(u_pool, pool_rows, axis=0).reshape(1, BATCH, POOL_BUF, d)
    sample_conv = jnp.concatenate([conv_buf, xbc_s.transpose(1, 0, 2)], axis=1)[:, -keep:][None]
    sample_pool = jnp.concatenate([pool_buf, u_pool_s.transpose(1, 0, 2)], axis=1)[:, -POOL_BUF:][None]
    prompt_ssm = h_p.reshape(1, BATCH, N_HEADS, HEAD_DIM, D_STATE)
    sample_ssm = h_s.reshape(1, DEC_BATCH, N_HEADS, HEAD_DIM, D_STATE)
    return (y_prompt, y_sample, prompt_ssm, prompt_conv, prompt_pool, sample_ssm, sample_conv, sample_pool)
```

```python
import functools

import jax
import jax.numpy as jnp
from jax import lax
from jax.experimental import pallas as pl
from jax.experimental.pallas import tpu as pltpu

F32 = jnp.float32
BF16 = jnp.bfloat16

D_MODEL = 2048
BATCH = 4
SEQ = 2048
DEC_BATCH = 128
DEC_SEQ = 4
PAST_LEN = 16384
POOL_WINDOWS = (2, 4, 8, 16)
POOL_GROUP = D_MODEL // len(POOL_WINDOWS)
POOL_BUF = max(POOL_WINDOWS) - 1
D_INNER = 2 * D_MODEL
HEAD_DIM = 64
N_HEADS = D_INNER // HEAD_DIM
D_STATE = 128
N_GROUPS = 8
HEADS_PER_GROUP = N_HEADS // N_GROUPS
GROUP_CH = D_INNER // N_GROUPS
CONV_W = 4
CONV_DIM = D_INNER + 2 * N_GROUPS * D_STATE
CHUNK = 128
D_FF = 256 * ((8 * D_MODEL // 3 + 255) // 256)
N_MOD = 9
EPS = 1e-6

N_PROMPT = BATCH * SEQ
N_SAMPLE = DEC_BATCH * DEC_SEQ
N_ROWS = N_PROMPT + N_SAMPLE
ROW_TILE = 512
N_ROW_TILES = N_ROWS // ROW_TILE
N_CHUNKS = SEQ // CHUNK
LANES = 128
PAIRS_PER_GROUP = HEADS_PER_GROUP // 2
SAMPLE_BB = 32
XBC_B0 = D_INNER
XBC_C0 = D_INNER + N_GROUPS * D_STATE
XBC_B_COL = XBC_B0 // D_STATE
XBC_C_COL = XBC_C0 // D_STATE
CONV_HALO = 8
MIB = 1024 * 1024


def _params(n_axes, vmem_mib):
    return pltpu.CompilerParams(dimension_semantics=("arbitrary",) * n_axes,
                                vmem_limit_bytes=vmem_mib * MIB)


def _silu(x):
    return x * jax.nn.sigmoid(x)


def _row_mods(mp_ref, ms_ref, tile, tm):
    seq = jnp.minimum(tile // (SEQ // tm), BATCH - 1)
    ms = ms_ref[...]
    ms = jnp.concatenate([ms] * (tm // DEC_BATCH), axis=0)
    return jnp.where(tile >= N_PROMPT // tm, ms, mp_ref[pl.ds(seq, 1), :])


def _rows2(xp_ref, xs_ref, tile, tm):
    return jnp.where(tile >= N_PROMPT // tm, xs_ref[...], xp_ref[...])


def _spec_p(tm, width, col=0):
    last = N_PROMPT // tm - 1
    return pl.BlockSpec((tm, width), lambda i: (jnp.minimum(i, last), col))


def _spec_s(tm, width, col=0):
    first = N_PROMPT // tm
    return pl.BlockSpec((tm, width), lambda i: (jnp.maximum(i - first, 0), col))


def _mod_specs(chunk):
    return [pl.BlockSpec((BATCH, D_MODEL), lambda i: (0, chunk)),
            pl.BlockSpec((DEC_BATCH, D_MODEL), lambda i: (0, chunk))]


def _rms(x):
    return x * lax.rsqrt(jnp.mean(x * x, axis=-1, keepdims=True) + EPS)


def _ada_kernel(c_ref, w_ref, b_ref, o_ref):
    a = _silu(c_ref[...]).astype(BF16)
    o_ref[...] = jnp.dot(a, w_ref[...].astype(BF16), preferred_element_type=F32) + b_ref[...]


def _ada(c_all, w, b):
    m = c_all.shape[0]
    n = w.shape[1]
    tn = 1024
    return pl.pallas_call(
        _ada_kernel,
        grid=(n // tn,),
        in_specs=[pl.BlockSpec((m, D_MODEL), lambda j: (0, 0)),
                  pl.BlockSpec((D_MODEL, tn), lambda j: (0, j)),
                  pl.BlockSpec((1, tn), lambda j: (0, j))],
        out_specs=pl.BlockSpec((m, tn), lambda j: (0, j)),
        out_shape=jax.ShapeDtypeStruct((m, n), F32),
        compiler_params=_params(1, 40),
        name="ada_mods",
    )(c_all, w, b)


def _normmod_kernel(xp_ref, xs_ref, g_ref, shp_ref, shs_ref, scp_ref, scs_ref, o_ref):
    i = pl.program_id(0)
    y = _rms(_rows2(xp_ref, xs_ref, i, ROW_TILE)) * g_ref[...]
    shift = _row_mods(shp_ref, shs_ref, i, ROW_TILE)
    scale = _row_mods(scp_ref, scs_ref, i, ROW_TILE)
    o_ref[...] = (y * (1.0 + scale) + shift).astype(BF16)


def _normmod(xp, xs, g, mods_p, mods_s, shift_chunk, scale_chunk):
    return pl.pallas_call(
        _normmod_kernel,
        grid=(N_ROW_TILES,),
        in_specs=[_spec_p(ROW_TILE, D_MODEL), _spec_s(ROW_TILE, D_MODEL),
                  pl.BlockSpec((1, D_MODEL), lambda i: (0, 0))]
                 + _mod_specs(shift_chunk) + _mod_specs(scale_chunk),
        out_specs=pl.BlockSpec((ROW_TILE, D_MODEL), lambda i: (i, 0)),
        out_shape=jax.ShapeDtypeStruct((N_ROWS, D_MODEL), BF16),
        compiler_params=_params(1, 48),
        name="norm_modulate",
    )(xp, xs, g, mods_p, mods_s, mods_p, mods_s)


def _ffn_up_kernel(a_ref, wa_ref, wb_ref, o_ref, wa_s, wb_s):
    @pl.when(pl.program_id(1) == 0)
    def _():
        wa_s[...] = wa_ref[...].astype(BF16)
        wb_s[...] = wb_ref[...].astype(BF16)

    a = a_ref[...]
    ha = jnp.dot(a, wa_s[...], preferred_element_type=F32)
    hb = jnp.dot(a, wb_s[...], preferred_element_type=F32)
    o_ref[...] = (_silu(ha) * hb).astype(BF16)


def _ffn_up(a, w13):
    tn = 512
    nt = D_FF // tn
    return pl.pallas_call(
        _ffn_up_kernel,
        grid=(nt, N_ROW_TILES),
        in_specs=[pl.BlockSpec((ROW_TILE, D_MODEL), lambda j, i: (i, 0)),
                  pl.BlockSpec((D_MODEL, tn), lambda j, i: (0, j)),
                  pl.BlockSpec((D_MODEL, tn), lambda j, i: (0, nt + j))],
        out_specs=pl.BlockSpec((ROW_TILE, tn), lambda j, i: (i, j)),
        out_shape=jax.ShapeDtypeStruct((N_ROWS, D_FF), BF16),
        scratch_shapes=[pltpu.VMEM((D_MODEL, tn), BF16), pltpu.VMEM((D_MODEL, tn), BF16)],
        compiler_params=_params(2, 48),
        name="ffn_up",
    )(a, w13, w13)


def _in_proj_kernel(a_ref, w_ref, o_ref, w_s, *, act):
    @pl.when(pl.program_id(1) == 0)
    def _():
        w_s[...] = w_ref[...].astype(BF16)

    acc = jnp.dot(a_ref[...], w_s[...], preferred_element_type=F32)
    if act is not None:
        acc = act(acc)
    o_ref[...] = acc.astype(o_ref.dtype)


def _in_proj(a, w, col0, ncols, tn, out_dtype, act=None):
    k = a.shape[1]
    nt = ncols // tn
    off = col0 // tn
    return pl.pallas_call(
        functools.partial(_in_proj_kernel, act=act),
        grid=(nt, N_ROW_TILES),
        in_specs=[pl.BlockSpec((ROW_TILE, k), lambda j, i: (i, 0)),
                  pl.BlockSpec((k, tn), lambda j, i: (0, off + j))],
        out_specs=pl.BlockSpec((ROW_TILE, tn), lambda j, i: (i, j)),
        out_shape=jax.ShapeDtypeStruct((N_ROWS, ncols), out_dtype),
        scratch_shapes=[pltpu.VMEM((k, tn), BF16)],
        compiler_params=_params(2, 48),
        name="in_proj",
    )(a, w)


def _load_weight(w_hbm, w_s, stage, sem, chunk):
    n = w_hbm.shape[0] // chunk

    def copy(c):
        return pltpu.make_async_copy(w_hbm.at[pl.ds(c * chunk, chunk), :], stage.at[c % 2], sem.at[c % 2])

    copy(0).start()
    for c in range(n):
        if c + 1 < n:
            copy(c + 1).start()
        copy(c).wait()
        w_s[c * chunk:(c + 1) * chunk, :] = stage[c % 2].astype(BF16)


def _res_block_kernel(*refs, tm, scale, two_source_res, final, chunk):
    refs = list(refs)
    a_ref, w_hbm = refs[:2]
    pos = 2
    if two_source_res:
        resp_ref, ress_ref = refs[pos:pos + 2]
        pos += 2
    else:
        res_ref = refs[pos]
        pos += 1
    gp_ref, gs_ref, g_ref = refs[pos:pos + 3]
    pos += 3
    if not final:
        shp_ref, shs_ref, scp_ref, scs_ref = refs[pos:pos + 4]
        pos += 4
    outs = refs[pos:pos + 2]
    w_s, stage, sem = refs[pos + 2:]
    i = pl.program_id(0)

    @pl.when(i == 0)
    def _():
        _load_weight(w_hbm, w_s, stage, sem, chunk)

    res = _rows2(resp_ref, ress_ref, i, tm) if two_source_res else res_ref[...]
    gate = _row_mods(gp_ref, gs_ref, i, tm)
    if scale != 1.0:
        gate = scale * gate
    h = res + gate * jnp.dot(a_ref[...], w_s[...], preferred_element_type=F32)
    y = _rms(h) * g_ref[...]
    if final:
        yp_ref, ys_ref = outs

        @pl.when(i < N_PROMPT // tm)
        def _():
            yp_ref[...] = y

        @pl.when(i >= N_PROMPT // tm)
        def _():
            ys_ref[...] = y
    else:
        h_ref, nxt_ref = outs
        h_ref[...] = h
        shift = _row_mods(shp_ref, shs_ref, i, tm)
        sc = _row_mods(scp_ref, scs_ref, i, tm)
        nxt_ref[...] = (y * (1.0 + sc) + shift).astype(BF16)


def _res_block(a, w, res, mods_p, mods_s, gate_chunk, scale, norm_g, next_chunks, tm, chunk, vmem_mib):
    k = a.shape[1]
    final = next_chunks is None
    two = isinstance(res, tuple)
    row = pl.BlockSpec((tm, D_MODEL), lambda i: (i, 0))
    in_specs = [pl.BlockSpec((tm, k), lambda i: (i, 0)), pl.BlockSpec(memory_space=pl.ANY)]
    args = [a, w]
    if two:
        in_specs += [_spec_p(tm, D_MODEL), _spec_s(tm, D_MODEL)]
        args += list(res)
    else:
        in_specs.append(row)
        args.append(res)
    in_specs += _mod_specs(gate_chunk) + [pl.BlockSpec((1, D_MODEL), lambda i: (0, 0))]
    args += [mods_p, mods_s, norm_g]
    if final:
        out_specs = [_spec_p(tm, D_MODEL), _spec_s(tm, D_MODEL)]
        out_shape = [jax.ShapeDtypeStruct((N_PROMPT, D_MODEL), F32), jax.ShapeDtypeStruct((N_SAMPLE, D_MODEL), F32)]
    else:
        in_specs += _mod_specs(next_chunks[0]) + _mod_specs(next_chunks[1])
        args += [mods_p, mods_s, mods_p, mods_s]
        out_specs = [row, row]
        out_shape = [jax.ShapeDtypeStruct((N_ROWS, D_MODEL), F32), jax.ShapeDtypeStruct((N_ROWS, D_MODEL), BF16)]
    return pl.pallas_call(
        functools.partial(_res_block_kernel, tm=tm, scale=scale, two_source_res=two, final=final, chunk=chunk),
        grid=(N_ROWS // tm,),
        in_specs=in_specs,
        out_specs=out_specs,
        out_shape=out_shape,
        scratch_shapes=[pltpu.VMEM((k, D_MODEL), BF16), pltpu.VMEM((2, chunk, D_MODEL), F32),
                        pltpu.SemaphoreType.DMA((2,))],
        compiler_params=_params(1, vmem_mib),
        name="res_block_final" if final else "res_block",
    )(*args)


def _merge_kernel(ypp_ref, yps_ref, ysp_ref, yss_ref, gp_ref, gs_ref, wp_hbm, ws_hbm, o_ref,
                  wp_s, ws_s, stage, sem, *, tm, chunk):
    i = pl.program_id(0)

    @pl.when(i == 0)
    def _():
        _load_weight(wp_hbm, wp_s, stage, sem, chunk)
        _load_weight(ws_hbm, ws_s, stage, sem, chunk)

    mp = jnp.dot(_rows2(ypp_ref, yps_ref, i, tm), wp_s[...], preferred_element_type=F32)
    ms = jnp.dot(_rows2(ysp_ref, yss_ref, i, tm), ws_s[...], preferred_element_type=F32)
    o_ref[...] = (gp_ref[...].astype(F32) * mp + gs_ref[...].astype(F32) * ms).astype(BF16)


def _merge(yp_p, yp_s, ys_p, ys_s, gates, w_bp, w_bs):
    tm, chunk = 256, 256
    return pl.pallas_call(
        functools.partial(_merge_kernel, tm=tm, chunk=chunk),
        grid=(N_ROWS // tm,),
        in_specs=[_spec_p(tm, D_MODEL), _spec_s(tm, D_MODEL), _spec_p(tm, D_INNER), _spec_s(tm, D_INNER),
                  pl.BlockSpec((tm, D_MODEL), lambda i: (i, 0)), pl.BlockSpec((tm, D_MODEL), lambda i: (i, 1)),
                  pl.BlockSpec(memory_space=pl.ANY), pl.BlockSpec(memory_space=pl.ANY)],
        out_specs=pl.BlockSpec((tm, D_MODEL), lambda i: (i, 0)),
        out_shape=jax.ShapeDtypeStruct((N_ROWS, D_MODEL), BF16),
        scratch_shapes=[pltpu.VMEM((D_MODEL, D_MODEL), BF16), pltpu.VMEM((D_INNER, D_MODEL), BF16),
                        pltpu.VMEM((2, chunk, D_MODEL), F32), pltpu.SemaphoreType.DMA((2,))],
        compiler_params=_params(1, 52),
        name="branch_merge",
    )(yp_p, yp_s, ys_p, ys_s, gates, gates, w_bp, w_bs)


def _pool_prompt_kernel(u_ref, uprev_ref, pw_ref, ps_ref, o_ref, stage):
    tiles_per_seq = SEQ // ROW_TILE
    lt = pl.program_id(0) % tiles_per_seq
    halo = POOL_BUF + 1
    stage[0:halo, :] = jnp.where(lt == 0, 0.0, uprev_ref[...])
    stage[halo:halo + ROW_TILE, :] = u_ref[...]
    t = lt * ROW_TILE + lax.broadcasted_iota(jnp.int32, (ROW_TILE, 1), 0)
    for g, w in enumerate(POOL_WINDOWS):
        lo = g * POOL_GROUP
        cur = u_ref[:, lo:lo + POOL_GROUP]
        s = cur
        for j in range(1, w):
            s = s + stage[halo - j:halo - j + ROW_TILE, lo:lo + POOL_GROUP]
        cnt = jnp.minimum(w, t + 1).astype(F32)
        pooled = (s / cnt - cur).astype(BF16)
        mixed = jnp.dot(pooled, pw_ref[g].astype(BF16), preferred_element_type=F32)
        o_ref[:, lo:lo + POOL_GROUP] = (mixed * ps_ref[:, lo:lo + POOL_GROUP]).astype(BF16)


def _pool_prompt(u_pool, pool_w, pool_scale):
    halo = POOL_BUF + 1
    per = ROW_TILE // halo
    return pl.pallas_call(
        _pool_prompt_kernel,
        grid=(N_PROMPT // ROW_TILE,),
        in_specs=[pl.BlockSpec((ROW_TILE, D_MODEL), lambda i: (i, 0)),
                  pl.BlockSpec((halo, D_MODEL), lambda i: (jnp.maximum(i * per - 1, 0), 0)),
                  pl.BlockSpec((len(POOL_WINDOWS), POOL_GROUP, POOL_GROUP), lambda i: (0, 0, 0)),
                  pl.BlockSpec((1, D_MODEL), lambda i: (0, 0))],
        out_specs=pl.BlockSpec((ROW_TILE, D_MODEL), lambda i: (i, 0)),
        out_shape=jax.ShapeDtypeStruct((N_PROMPT, D_MODEL), BF16),
        scratch_shapes=[pltpu.VMEM((halo + ROW_TILE, D_MODEL), F32)],
        compiler_params=_params(1, 48),
        name="pool_prompt",
    )(u_pool, u_pool, pool_w, pool_scale)


def _pool_sample_kernel(u_ref, buf_ref, pw_ref, ps_ref, o_ref):
    n_prev = min(PAST_LEN, POOL_BUF)
    for g, w in enumerate(POOL_WINDOWS):
        lo = g * POOL_GROUP

        def row(k):
            if k < POOL_BUF:
                return buf_ref[k, :, lo:lo + POOL_GROUP]
            return u_ref[k - POOL_BUF, :, lo:lo + POOL_GROUP]

        pooled = []
        for t in range(DEC_SEQ):
            cur = row(POOL_BUF + t)
            s = cur
            for j in range(1, w):
                s = s + row(POOL_BUF + t - j)
            cnt = float(min(w, t + 1 + n_prev))
            pooled.append(s / cnt - cur)
        pooled = jnp.concatenate(pooled, axis=0).astype(BF16)
        mixed = jnp.dot(pooled, pw_ref[g].astype(BF16), preferred_element_type=F32)
        y = (mixed * ps_ref[:, lo:lo + POOL_GROUP]).astype(BF16)
        for t in range(DEC_SEQ):
            o_ref[t, :, lo:lo + POOL_GROUP] = y[t * SAMPLE_BB:(t + 1) * SAMPLE_BB]


def _pool_sample(u_s, buf_t, pool_w, pool_scale):
    return pl.pallas_call(
        _pool_sample_kernel,
        grid=(DEC_BATCH // SAMPLE_BB,),
        in_specs=[pl.BlockSpec((DEC_SEQ, SAMPLE_BB, D_MODEL), lambda i: (0, i, 0)),
                  pl.BlockSpec((POOL_BUF, SAMPLE_BB, D_MODEL), lambda i: (0, i, 0)),
                  pl.BlockSpec((len(POOL_WINDOWS), POOL_GROUP, POOL_GROUP), lambda i: (0, 0, 0)),
                  pl.BlockSpec((1, D_MODEL), lambda i: (0, 0))],
        out_specs=pl.BlockSpec((DEC_SEQ, SAMPLE_BB, D_MODEL), lambda i: (0, i, 0)),
        out_shape=jax.ShapeDtypeStruct((DEC_SEQ, DEC_BATCH, D_MODEL), BF16),
        compiler_params=_params(1, 40),
        name="pool_sample",
    )(u_s, buf_t, pool_w, pool_scale)


def _gated_group_norm(y, x, zs, dskip, norm):
    y = (y + x * dskip) * zs.astype(F32)
    return (_rms(y) * norm).astype(BF16)


def _ssd_prompt_kernel(xbc_ref, prev_ref, zs_ref, dt_ref, dtb_ref, alog_ref, cw_ref, cb_ref, dskip_ref, norm_ref,
                       y_ref, hout_ref, h_s, xa_ref):
    c = pl.program_id(1)
    first = c == 0
    halo = CONV_HALO

    @pl.when(first)
    def _():
        h_s[...] = jnp.zeros_like(h_s)

    slab = 512
    for lo in range(0, CONV_DIM, slab):
        cols = slice(lo, lo + slab)
        ext = jnp.concatenate([jnp.where(first, 0.0, prev_ref[:, cols]), xbc_ref[:, cols]], axis=0)
        acc = ext * cw_ref[0:1, cols]
        for k in range(1, CONV_W):
            acc = pltpu.roll(acc, 1, axis=0) + ext * cw_ref[k:k + 1, cols]
        xa_ref[:, cols] = _silu(acc[halo:halo + CHUNK] + cb_ref[:, cols]).astype(BF16)

    dt_t = jax.nn.softplus(dt_ref[...] + dtb_ref[...])
    la_t = dt_t * (-jnp.exp(alog_ref[...]))
    lane_t = lax.broadcasted_iota(jnp.int32, la_t.shape, 1)
    s_t = la_t
    k = 1
    while k < CHUNK:
        s_t = s_t + jnp.where(lane_t >= k, pltpu.roll(s_t, k, axis=1), 0.0)
        k *= 2
    log2e = 1.0 / jnp.log(2.0)
    s2_t = s_t * log2e
    r2_t = (s_t - jnp.log(dt_t)) * log2e
    tail2_t = jnp.exp2(s2_t[:, CHUNK - 1:CHUNK] - r2_t)

    tri = (lax.broadcasted_iota(jnp.int32, (CHUNK, CHUNK), 0)
           >= lax.broadcasted_iota(jnp.int32, (CHUNK, CHUNK), 1))
    lo_half = lax.broadcasted_iota(jnp.int32, (CHUNK, LANES), 1) < HEAD_DIM

    for g in range(N_GROUPS):
        x = xa_ref[:, g * GROUP_CH:(g + 1) * GROUP_CH].astype(F32)
        bm_b = xa_ref[:, XBC_B0 + g * D_STATE:XBC_B0 + (g + 1) * D_STATE]
        cm_b = xa_ref[:, XBC_C0 + g * D_STATE:XBC_C0 + (g + 1) * D_STATE]
        cb = lax.dot_general(cm_b, bm_b, (((1,), (1,)), ((), ())), preferred_element_type=F32)
        bm_t = bm_b.astype(F32).T
        h_old = h_s[g]
        inter = jnp.dot(cm_b, h_old.astype(BF16), preferred_element_type=F32)

        ys, hs = [], []
        for i in range(PAIRS_PER_GROUP):
            w_pair, bt_pair, col_pair = [], [], []
            for j in (g * HEADS_PER_GROUP + 2 * i, g * HEADS_PER_GROUP + 2 * i + 1):
                colb = jnp.broadcast_to(s2_t[j:j + 1, :], (CHUNK, CHUNK)).T
                expo = jnp.where(tri, colb - r2_t[j:j + 1, :], -jnp.inf)
                w_pair.append((cb * jnp.exp2(expo)).astype(BF16))
                bt_pair.append((bm_t * tail2_t[j:j + 1, :]).astype(BF16))
                col_pair.append(colb)
            xp = x[:, i * LANES:(i + 1) * LANES]
            rhs = jnp.concatenate([jnp.where(lo_half, xp, 0.0), jnp.where(lo_half, 0.0, xp)], axis=0).astype(BF16)
            lhs = jnp.concatenate([jnp.concatenate(w_pair, axis=1), jnp.concatenate(bt_pair, axis=1)], axis=0)
            out = jnp.dot(lhs, rhs, preferred_element_type=F32)
            e_sel = jnp.exp2(jnp.where(lo_half, col_pair[0], col_pair[1]))
            ys.append(out[0:CHUNK] + inter[:, i * LANES:(i + 1) * LANES] * e_sel)
            hs.append(h_old[:, i * LANES:(i + 1) * LANES] * e_sel[CHUNK - 1:CHUNK, :] + out[CHUNK:2 * CHUNK])
        h_s[g] = jnp.concatenate(hs, axis=1)
        cols = slice(g * GROUP_CH, (g + 1) * GROUP_CH)
        y_ref[:, cols] = _gated_group_norm(jnp.concatenate(ys, axis=1), x, zs_ref[:, cols],
                                           dskip_ref[:, cols], norm_ref[:, cols])

    @pl.when(c == N_CHUNKS - 1)
    def _():
        for g in range(N_GROUPS):
            for i in range(PAIRS_PER_GROUP):
                pair = g * PAIRS_PER_GROUP + i
                h_pair = h_s[g, :, i * LANES:(i + 1) * LANES].T
                hout_ref[0, 0, 2 * pair:2 * pair + 2] = h_pair.reshape(2, HEAD_DIM, D_STATE)


def _ssd_prompt(xbc, zs, dt_t, dt_bias, a_log, conv_w, conv_b, dskip_e, ssm_norm):
    per = CHUNK // CONV_HALO

    def full(rows, width):
        return pl.BlockSpec((rows, width), lambda b, c: (0, 0))

    y, h = pl.pallas_call(
        _ssd_prompt_kernel,
        grid=(BATCH, N_CHUNKS),
        in_specs=[pl.BlockSpec((CHUNK, CONV_DIM), lambda b, c: (b * N_CHUNKS + c, 0)),
                  pl.BlockSpec((CONV_HALO, CONV_DIM),
                               lambda b, c: (jnp.maximum((b * N_CHUNKS + c) * per - 1, 0), 0)),
                  pl.BlockSpec((CHUNK, D_INNER), lambda b, c: (b * N_CHUNKS + c, 0)),
                  pl.BlockSpec((N_HEADS, CHUNK), lambda b, c: (0, b * N_CHUNKS + c)),
                  full(N_HEADS, 1), full(N_HEADS, 1),
                  full(CONV_W, CONV_DIM), full(1, CONV_DIM), full(1, D_INNER), full(1, D_INNER)],
        out_specs=[pl.BlockSpec((CHUNK, D_INNER), lambda b, c: (b * N_CHUNKS + c, 0)),
                   pl.BlockSpec((1, 1, N_HEADS, HEAD_DIM, D_STATE), lambda b, c: (0, b, 0, 0, 0))],
        out_shape=[jax.ShapeDtypeStruct((N_PROMPT, D_INNER), BF16),
                   jax.ShapeDtypeStruct((1, BATCH, N_HEADS, HEAD_DIM, D_STATE), F32)],
        scratch_shapes=[pltpu.VMEM((N_GROUPS, D_STATE, GROUP_CH), F32), pltpu.VMEM((CHUNK, CONV_DIM), BF16)],
        compiler_params=_params(2, 40),
        name="ssd_prompt",
    )(xbc, xbc, zs, dt_t, dt_bias, a_log, conv_w, conv_b, dskip_e, ssm_norm)
    return y, h


def _split3(v):
    hi = v.astype(BF16)
    r1 = v - hi.astype(F32)
    mid = r1.astype(BF16)
    lo = (r1 - mid.astype(F32)).astype(BF16)
    return hi, mid, lo


def _ssd_sample_kernel(xh_ref, xst_ref, bm_ref, bst_ref, cm_ref, cst_ref, zs_ref, dt_ref, dtb_ref, alog_ref,
                       cwx_ref, cbx_ref, cwb_ref, cbb_ref, cwc_ref, cbc_ref, dskip_ref, norm_ref, h0_ref,
                       y_ref, hout_ref):
    g = pl.program_id(1)
    bb = SAMPLE_BB
    rows = DEC_SEQ * bb

    def conv_silu(cur_ref, st_ref, w_ref, b_ref):
        full = [st_ref[k] for k in range(CONV_W - 1)] + [cur_ref[t] for t in range(DEC_SEQ)]
        outs = []
        for t in range(DEC_SEQ):
            acc = b_ref[...] + full[t] * w_ref[0:1, :]
            for k in range(1, CONV_W):
                acc = acc + full[t + k] * w_ref[k:k + 1, :]
            outs.append(_silu(acc))
        return outs

    x = conv_silu(xh_ref, xst_ref, cwx_ref, cbx_ref)
    bm = conv_silu(bm_ref, bst_ref, cwb_ref, cbb_ref)
    cm = conv_silu(cm_ref, cst_ref, cwc_ref, cbc_ref)

    a = -jnp.exp(alog_ref[...])
    dt = [jax.nn.softplus(dt_ref[t] + dtb_ref[...]) for t in range(DEC_SEQ)]
    s = [dt[0] * a]
    for t in range(1, DEC_SEQ):
        s.append(s[t - 1] + dt[t] * a)
    sel = (lax.broadcasted_iota(jnp.int32, (LANES, GROUP_CH), 0)
           == g * HEADS_PER_GROUP + lax.broadcasted_iota(jnp.int32, (LANES, GROUP_CH), 1) // HEAD_DIM)
    sel = jnp.where(sel, 1.0, 0.0).astype(BF16)

    def expand(v):
        return sum(jnp.dot(p, sel, preferred_element_type=F32) for p in _split3(v))

    dt_e = expand(jnp.concatenate(dt, axis=0))
    s_e = expand(jnp.concatenate(s, axis=0))
    dt_e = [dt_e[t * bb:(t + 1) * bb] for t in range(DEC_SEQ)]
    s_e = [s_e[t * bb:(t + 1) * bb] for t in range(DEC_SEQ)]

    c_all = jnp.concatenate(cm, axis=0).astype(BF16)
    b_all = jnp.concatenate(bm, axis=0)
    last = DEC_SEQ - 1
    xw_all = jnp.concatenate([x[t] * dt_e[t] * jnp.exp(s_e[last] - s_e[t]) for t in range(DEC_SEQ)], axis=0)
    xw_t = xw_all.T.astype(BF16)
    dec_t = jnp.concatenate([jnp.exp(s_e[last])] + [jnp.zeros((bb, GROUP_CH), F32)] * last, axis=0).T
    row_b = lax.broadcasted_iota(jnp.int32, (rows, 1), 0) % bb
    inter = jnp.zeros((rows, GROUP_CH), F32)
    for b in range(bb):
        mine = row_b == b
        h0 = h0_ref[0, b].reshape(GROUP_CH, D_STATE)
        yb = lax.dot_general(c_all, h0.astype(BF16), (((1,), (1,)), ((), ())), preferred_element_type=F32)
        inter = inter + jnp.where(mine, yb, 0.0)
        dh = jnp.dot(xw_t, jnp.where(mine, b_all, 0.0).astype(BF16), preferred_element_type=F32)
        h_new = h0 * jnp.broadcast_to(dec_t[:, b:b + 1], (GROUP_CH, D_STATE)) + dh
        hout_ref[0, b] = h_new.reshape(HEADS_PER_GROUP, HEAD_DIM, D_STATE)

    for t in range(DEC_SEQ):
        y = inter[t * bb:(t + 1) * bb] * jnp.exp(s_e[t])
        for u in range(t + 1):
            cb = jnp.sum(cm[t] * bm[u], axis=-1, keepdims=True)
            y = y + cb * jnp.exp(s_e[t] - s_e[u]) * dt_e[u] * x[u]
        y_ref[t] = _gated_group_norm(y, x[t], zs_ref[t], dskip_ref[...], norm_ref[...])


def _ssd_sample(xbc_s, conv_st, zs_s, dt_s, dt_bias_row, a_log_row, conv_w, conv_b, dskip_e, ssm_norm, h0):
    bb = SAMPLE_BB

    def cur(width, col0):
        return pl.BlockSpec((DEC_SEQ, bb, width), lambda i, g: (0, i, col0 + g))

    def st(width, col0):
        return pl.BlockSpec((CONV_W - 1, bb, width), lambda i, g: (0, i, col0 + g))

    def par(rows, width, col0):
        return pl.BlockSpec((rows, width), lambda i, g: (0, col0 + g))

    head_row = pl.BlockSpec((1, LANES), lambda i, g: (0, 0))
    state = pl.BlockSpec((1, bb, HEADS_PER_GROUP, HEAD_DIM, D_STATE), lambda i, g: (0, i, g, 0, 0))
    y, h = pl.pallas_call(
        _ssd_sample_kernel,
        grid=(DEC_BATCH // bb, N_GROUPS),
        in_specs=[cur(GROUP_CH, 0), st(GROUP_CH, 0),
                  cur(D_STATE, XBC_B_COL), st(D_STATE, XBC_B_COL),
                  cur(D_STATE, XBC_C_COL), st(D_STATE, XBC_C_COL),
                  cur(GROUP_CH, 0),
                  pl.BlockSpec((DEC_SEQ, bb, LANES), lambda i, g: (0, i, 0)),
                  head_row, head_row,
                  par(CONV_W, GROUP_CH, 0), par(1, GROUP_CH, 0),
                  par(CONV_W, D_STATE, XBC_B_COL), par(1, D_STATE, XBC_B_COL),
                  par(CONV_W, D_STATE, XBC_C_COL), par(1, D_STATE, XBC_C_COL),
                  par(1, GROUP_CH, 0), par(1, GROUP_CH, 0), state],
        out_specs=[pl.BlockSpec((DEC_SEQ, bb, GROUP_CH), lambda i, g: (0, i, g)), state],
        out_shape=[jax.ShapeDtypeStruct((DEC_SEQ, DEC_BATCH, D_INNER), BF16),
                   jax.ShapeDtypeStruct((1, DEC_BATCH, N_HEADS, HEAD_DIM, D_STATE), F32)],
        compiler_params=_params(2, 56),
        name="ssd_sample",
    )(xbc_s, conv_st, xbc_s, conv_st, xbc_s, conv_st, zs_s, dt_s, dt_bias_row, a_log_row,
      conv_w, conv_b, conv_w, conv_b, conv_w, conv_b, dskip_e, ssm_norm, h0)
    return y, h


def kernel(x_prompt, x_sample, c_prompt, c_sample, state_ssm, state_conv, state_pool, w_ada, b_ada, norm_ffn1,
           w13_ffn1, w2_ffn1, norm_mix, w_in, pool_w, pool_scale, conv_w, conv_b, dt_bias, a_log, d_skip,
           ssm_norm, w_branch_pool, w_branch_ssm, w_out, norm_ffn2, w13_ffn2, w2_ffn2, norm_final):
    d = D_MODEL

    def layer0(w):
        return w.reshape(w.shape[1:])

    def row(v):
        return v.reshape(1, -1)

    xp = x_prompt.reshape(N_PROMPT, d)
    xs = x_sample.transpose(1, 0, 2).reshape(N_SAMPLE, d)

    n_c = BATCH + DEC_BATCH
    c_all = jnp.pad(jnp.concatenate([c_prompt, c_sample], axis=0), ((0, -n_c % 8), (0, 0)))
    mods = _ada(c_all, layer0(w_ada), b_ada.reshape(1, N_MOD * d))
    mods_p = mods[:BATCH]
    mods_s = mods[BATCH:n_c]

    a1 = _normmod(xp, xs, row(norm_ffn1), mods_p, mods_s, 0, 1)
    act1 = _ffn_up(a1, layer0(w13_ffn1))
    h1, u = _res_block(act1, layer0(w2_ffn1), (xp, xs), mods_p, mods_s, 2, 0.5, row(norm_mix), (3, 4),
                       tm=256, chunk=176, vmem_mib=58)

    w_in0 = layer0(w_in)
    col_z, col_xbc, col_dt = d, d + D_INNER, d + D_INNER + CONV_DIM
    conv_buf = layer0(state_conv)
    u_pool = _in_proj(u, w_in0, 0, d, 1024, F32)
    zs = _in_proj(u, w_in0, col_z, D_INNER, 1024, BF16, act=_silu)
    conv_w0 = layer0(conv_w)
    conv_b0 = row(conv_b)
    xbc = _in_proj(u, w_in0, col_xbc, CONV_DIM, 1024, F32)
    gates = _in_proj(u, w_in0[:, col_dt + N_HEADS:], 0, 2 * d, 1024, BF16, act=jax.nn.sigmoid)
    dt_raw = _in_proj(u, jnp.pad(w_in0[:, col_dt:col_dt + N_HEADS], ((0, 0), (0, LANES - N_HEADS))),
                      0, LANES, LANES, F32)

    pw = layer0(pool_w)
    ps = row(pool_scale)
    u_pool_s = u_pool[N_PROMPT:].reshape(DEC_SEQ, DEC_BATCH, d)
    pool_buf = layer0(state_pool)
    y_pool_p = _pool_prompt(u_pool, pw, ps)
    y_pool_s = _pool_sample(u_pool_s, pool_buf.transpose(1, 0, 2), pw, ps).reshape(N_SAMPLE, d)

    dskip_e = row(jnp.repeat(d_skip.reshape(N_HEADS), HEAD_DIM))
    norm_row = row(ssm_norm)
    y_ssm_p, prompt_ssm = _ssd_prompt(xbc, zs, dt_raw[:N_PROMPT, :N_HEADS].T, dt_bias.reshape(N_HEADS, 1),
                                      a_log.reshape(N_HEADS, 1), conv_w0, conv_b0, dskip_e, norm_row)
    head_pad = ((0, 0), (0, LANES - N_HEADS))
    xbc_s = xbc[N_PROMPT:].reshape(DEC_SEQ, DEC_BATCH, CONV_DIM)
    y_ssm_s, sample_ssm = _ssd_sample(
        xbc_s, conv_buf.transpose(1, 0, 2), zs[N_PROMPT:].reshape(DEC_SEQ, DEC_BATCH, D_INNER),
        dt_raw[N_PROMPT:].reshape(DEC_SEQ, DEC_BATCH, LANES),
        jnp.pad(row(dt_bias), head_pad), jnp.pad(row(a_log), head_pad),
        conv_w0, conv_b0, dskip_e, norm_row, state_ssm)

    merged = _merge(y_pool_p, y_pool_s, y_ssm_p, y_ssm_s.reshape(N_SAMPLE, D_INNER), gates,
                    layer0(w_branch_pool), layer0(w_branch_ssm))
    h2, a3 = _res_block(merged, layer0(w_out), h1, mods_p, mods_s, 5, 1.0, row(norm_ffn2), (6, 7),
                        tm=512, chunk=256, vmem_mib=52)
    act2 = _ffn_up(a3, layer0(w13_ffn2))
    y_p, y_s = _res_block(act2, layer0(w2_ffn2), h2, mods_p, mods_s, 8, 0.5, row(norm_final), None,
                          tm=256, chunk=176, vmem_mib=56)

    y_prompt = y_p.reshape(BATCH, SEQ, d)
    y_sample = y_s.reshape(DEC_SEQ, DEC_BATCH, d).transpose(1, 0, 2)
    keep = CONV_W - 1
    seq_end = (jnp.arange(BATCH) + 1) * SEQ
    conv_rows = (seq_end[:, None] - keep + jnp.arange(keep)[None, :]).reshape(-1)
    pool_rows = (seq_end[:, None] - POOL_BUF + jnp.arange(POOL_BUF)[None, :]).reshape(-1)
    prompt_conv = jnp.take(xbc, conv_rows, axis=0).reshape(1, BATCH, keep, CONV_DIM)
    prompt_pool = jnp.take(u_pool, pool_rows, axis=0).reshape(1, BATCH, POOL_BUF, d)
    sample_conv = jnp.concatenate([conv_buf, xbc_s.transpose(1, 0, 2)], axis=1)[:, -keep:][None]
    sample_pool = jnp.concatenate([pool_buf, u_pool_s.transpose(1, 0, 2)], axis=1)[:, -POOL_BUF:][None]
    return (y_prompt, y_sample, prompt_ssm, prompt_conv, prompt_pool, sample_ssm, sample_conv, sample_pool)
```

```python
import functools

import jax
import jax.numpy as jnp
from jax import lax
from jax.experimental import pallas as pl
from jax.experimental.pallas import tpu as pltpu

F32 = jnp.float32
BF16 = jnp.bfloat16

D_MODEL = 2048
BATCH = 4
SEQ = 2048
DEC_BATCH = 128
DEC_SEQ = 4
PAST_LEN = 16384
POOL_WINDOWS = (2, 4, 8, 16)
POOL_GROUP = D_MODEL // len(POOL_WINDOWS)
POOL_BUF = max(POOL_WINDOWS) - 1
D_INNER = 2 * D_MODEL
HEAD_DIM = 64
N_HEADS = D_INNER // HEAD_DIM
D_STATE = 128
N_GROUPS = 8
HEADS_PER_GROUP = N_HEADS // N_GROUPS
GROUP_CH = D_INNER // N_GROUPS
CONV_W = 4
CONV_DIM = D_INNER + 2 * N_GROUPS * D_STATE
CHUNK = 128
D_FF = 256 * ((8 * D_MODEL // 3 + 255) // 256)
N_MOD = 9
EPS = 1e-6

N_PROMPT = BATCH * SEQ
N_SAMPLE = DEC_BATCH * DEC_SEQ
N_ROWS = N_PROMPT + N_SAMPLE
ROW_TILE = 512
N_ROW_TILES = N_ROWS // ROW_TILE
N_CHUNKS = SEQ // CHUNK
LANES = 128
PAIRS_PER_GROUP = HEADS_PER_GROUP // 2
SAMPLE_BB = 32
XBC_B0 = D_INNER
XBC_C0 = D_INNER + N_GROUPS * D_STATE
XBC_B_COL = XBC_B0 // D_STATE
XBC_C_COL = XBC_C0 // D_STATE
CONV_HALO = 8
MIB = 1024 * 1024


def _params(n_axes, vmem_mib):
    return pltpu.CompilerParams(dimension_semantics=("arbitrary",) * n_axes,
                                vmem_limit_bytes=vmem_mib * MIB)


def _silu(x):
    return x * jax.nn.sigmoid(x)


def _row_mods(mp_ref, ms_ref, tile, tm):
    seq = jnp.minimum(tile // (SEQ // tm), BATCH - 1)
    ms = ms_ref[...]
    ms = jnp.concatenate([ms] * (tm // DEC_BATCH), axis=0)
    return jnp.where(tile >= N_PROMPT // tm, ms, mp_ref[pl.ds(seq, 1), :])


def _rows2(xp_ref, xs_ref, tile, tm):
    return jnp.where(tile >= N_PROMPT // tm, xs_ref[...], xp_ref[...])


def _spec_p(tm, width, col=0):
    last = N_PROMPT // tm - 1
    return pl.BlockSpec((tm, width), lambda i: (jnp.minimum(i, last), col))


def _spec_s(tm, width, col=0):
    first = N_PROMPT // tm
    return pl.BlockSpec((tm, width), lambda i: (jnp.maximum(i - first, 0), col))


def _mod_specs(chunk):
    return [pl.BlockSpec((BATCH, D_MODEL), lambda i: (0, chunk)),
            pl.BlockSpec((DEC_BATCH, D_MODEL), lambda i: (0, chunk))]


def _rms(x):
    return x * lax.rsqrt(jnp.mean(x * x, axis=-1, keepdims=True) + EPS)


def _ada_kernel(c_ref, w_ref, b_ref, o_ref):
    a = _silu(c_ref[...]).astype(BF16)
    o_ref[...] = jnp.dot(a, w_ref[...].astype(BF16), preferred_element_type=F32) + b_ref[...]


def _ada(c_all, w, b):
    m = c_all.shape[0]
    n = w.shape[1]
    tn = 1024
    return pl.pallas_call(
        _ada_kernel,
        grid=(n // tn,),
        in_specs=[pl.BlockSpec((m, D_MODEL), lambda j: (0, 0)),
                  pl.BlockSpec((D_MODEL, tn), lambda j: (0, j)),
                  pl.BlockSpec((1, tn), lambda j: (0, j))],
        out_specs=pl.BlockSpec((m, tn), lambda j: (0, j)),
        out_shape=jax.ShapeDtypeStruct((m, n), F32),
        compiler_params=_params(1, 40),
        name="ada_mods",
    )(c_all, w, b)


def _normmod_kernel(xp_ref, xs_ref, g_ref, shp_ref, shs_ref, scp_ref, scs_ref, o_ref):
    i = pl.program_id(0)
    y = _rms(_rows2(xp_ref, xs_ref, i, ROW_TILE)) * g_ref[...]
    shift = _row_mods(shp_ref, shs_ref, i, ROW_TILE)
    scale = _row_mods(scp_ref, scs_ref, i, ROW_TILE)
    o_ref[...] = (y * (1.0 + scale) + shift).astype(BF16)


def _normmod(xp, xs, g, mods_p, mods_s, shift_chunk, scale_chunk):
    return pl.pallas_call(
        _normmod_kernel,
        grid=(N_ROW_TILES,),
        in_specs=[_spec_p(ROW_TILE, D_MODEL), _spec_s(ROW_TILE, D_MODEL),
                  pl.BlockSpec((1, D_MODEL), lambda i: (0, 0))]
                 + _mod_specs(shift_chunk) + _mod_specs(scale_chunk),
        out_specs=pl.BlockSpec((ROW_TILE, D_MODEL), lambda i: (i, 0)),
        out_shape=jax.ShapeDtypeStruct((N_ROWS, D_MODEL), BF16),
        compiler_params=_params(1, 48),
        name="norm_modulate",
    )(xp, xs, g, mods_p, mods_s, mods_p, mods_s)


def _ffn_up_kernel(a_ref, wa_ref, wb_ref, o_ref, wa_s, wb_s):
    @pl.when(pl.program_id(1) == 0)
    def _():
        wa_s[...] = wa_ref[...].astype(BF16)
        wb_s[...] = wb_ref[...].astype(BF16)

    a = a_ref[...]
    ha = jnp.dot(a, wa_s[...], preferred_element_type=F32)
    hb = jnp.dot(a, wb_s[...], preferred_element_type=F32)
    o_ref[...] = (_silu(ha) * hb).astype(BF16)


def _ffn_up(a, w13):
    tn = 512
    nt = D_FF // tn
    return pl.pallas_call(
        _ffn_up_kernel,
        grid=(nt, N_ROW_TILES),
        in_specs=[pl.BlockSpec((ROW_TILE, D_MODEL), lambda j, i: (i, 0)),
                  pl.BlockSpec((D_MODEL, tn), lambda j, i: (0, j)),
                  pl.BlockSpec((D_MODEL, tn), lambda j, i: (0, nt + j))],
        out_specs=pl.BlockSpec((ROW_TILE, tn), lambda j, i: (i, j)),
        out_shape=jax.ShapeDtypeStruct((N_ROWS, D_FF), BF16),
        scratch_shapes=[pltpu.VMEM((D_MODEL, tn), BF16), pltpu.VMEM((D_MODEL, tn), BF16)],
        compiler_params=_params(2, 48),
        name="ffn_up",
    )(a, w13, w13)


def _in_proj_kernel(a_ref, wt_ref, o_ref, wt_s, *, act):
    @pl.when(pl.program_id(1) == 0)
    def _():
        wt_s[...] = wt_ref[...].astype(BF16)

    acc = lax.dot_general(a_ref[...], wt_s[...], (((1,), (1,)), ((), ())), preferred_element_type=F32)
    if act is not None:
        acc = act(acc)
    o_ref[...] = acc.astype(o_ref.dtype)


def _in_proj(a, wt, row0, nrows, tn, out_dtype, act=None):
    k = a.shape[1]
    if row0 % tn == 0:
        w_spec = pl.BlockSpec((tn, k), lambda j, i: (row0 // tn + j, 0))
    else:
        assert row0 % 8 == 0
        w_spec = pl.BlockSpec((pl.Element(tn), pl.Element(k)),
                              lambda j, i: (pl.multiple_of(row0 + j * tn, 8), 0))
    return pl.pallas_call(
        functools.partial(_in_proj_kernel, act=act),
        grid=(nrows // tn, N_ROW_TILES),
        in_specs=[pl.BlockSpec((ROW_TILE, k), lambda j, i: (i, 0)), w_spec],
        out_specs=pl.BlockSpec((ROW_TILE, tn), lambda j, i: (i, j)),
        out_shape=jax.ShapeDtypeStruct((N_ROWS, nrows), out_dtype),
        scratch_shapes=[pltpu.VMEM((tn, k), BF16)],
        compiler_params=_params(2, 48),
        name="in_proj",
    )(a, wt)


def _load_weight(w_hbm, w_s, stage, sem, chunk):
    n = w_hbm.shape[0] // chunk

    def copy(c):
        return pltpu.make_async_copy(w_hbm.at[pl.ds(c * chunk, chunk), :], stage.at[c % 2], sem.at[c % 2])

    copy(0).start()
    for c in range(n):
        if c + 1 < n:
            copy(c + 1).start()
        copy(c).wait()
        w_s[c * chunk:(c + 1) * chunk, :] = stage[c % 2].astype(BF16)


def _res_block_kernel(*refs, tm, scale, two_source_res, final, chunk):
    refs = list(refs)
    a_ref, w_hbm = refs[:2]
    pos = 2
    if two_source_res:
        resp_ref, ress_ref = refs[pos:pos + 2]
        pos += 2
    else:
        res_ref = refs[pos]
        pos += 1
    gp_ref, gs_ref, g_ref = refs[pos:pos + 3]
    pos += 3
    if not final:
        shp_ref, shs_ref, scp_ref, scs_ref = refs[pos:pos + 4]
        pos += 4
    outs = refs[pos:pos + 2]
    w_s, stage, sem = refs[pos + 2:]
    i = pl.program_id(0)

    @pl.when(i == 0)
    def _():
        _load_weight(w_hbm, w_s, stage, sem, chunk)

    res = _rows2(resp_ref, ress_ref, i, tm) if two_source_res else res_ref[...]
    gate = _row_mods(gp_ref, gs_ref, i, tm)
    if scale != 1.0:
        gate = scale * gate
    h = res + gate * jnp.dot(a_ref[...], w_s[...], preferred_element_type=F32)
    y = _rms(h) * g_ref[...]
    if final:
        yp_ref, ys_ref = outs

        @pl.when(i < N_PROMPT // tm)
        def _():
            yp_ref[...] = y

        @pl.when(i >= N_PROMPT // tm)
        def _():
            ys_ref[...] = y
    else:
        h_ref, nxt_ref = outs
        h_ref[...] = h
        shift = _row_mods(shp_ref, shs_ref, i, tm)
        sc = _row_mods(scp_ref, scs_ref, i, tm)
        nxt_ref[...] = (y * (1.0 + sc) + shift).astype(BF16)


def _res_block(a, w, res, mods_p, mods_s, gate_chunk, scale, norm_g, next_chunks, tm, chunk, vmem_mib):
    k = a.shape[1]
    final = next_chunks is None
    two = isinstance(res, tuple)
    row = pl.BlockSpec((tm, D_MODEL), lambda i: (i, 0))
    in_specs = [pl.BlockSpec((tm, k), lambda i: (i, 0)), pl.BlockSpec(memory_space=pl.ANY)]
    args = [a, w]
    if two:
        in_specs += [_spec_p(tm, D_MODEL), _spec_s(tm, D_MODEL)]
        args += list(res)
    else:
        in_specs.append(row)
        args.append(res)
    in_specs += _mod_specs(gate_chunk) + [pl.BlockSpec((1, D_MODEL), lambda i: (0, 0))]
    args += [mods_p, mods_s, norm_g]
    if final:
        out_specs = [_spec_p(tm, D_MODEL), _spec_s(tm, D_MODEL)]
        out_shape = [jax.ShapeDtypeStruct((N_PROMPT, D_MODEL), F32), jax.ShapeDtypeStruct((N_SAMPLE, D_MODEL), F32)]
    else:
        in_specs += _mod_specs(next_chunks[0]) + _mod_specs(next_chunks[1])
        args += [mods_p, mods_s, mods_p, mods_s]
        out_specs = [row, row]
        out_shape = [jax.ShapeDtypeStruct((N_ROWS, D_MODEL), F32), jax.ShapeDtypeStruct((N_ROWS, D_MODEL), BF16)]
    return pl.pallas_call(
        functools.partial(_res_block_kernel, tm=tm, scale=scale, two_source_res=two, final=final, chunk=chunk),
        grid=(N_ROWS // tm,),
        in_specs=in_specs,
        out_specs=out_specs,
        out_shape=out_shape,
        scratch_shapes=[pltpu.VMEM((k, D_MODEL), BF16), pltpu.VMEM((2, chunk, D_MODEL), F32),
                        pltpu.SemaphoreType.DMA((2,))],
        compiler_params=_params(1, vmem_mib),
        name="res_block_final" if final else "res_block",
    )(*args)


def _merge_kernel(ypp_ref, yps_ref, ysp_ref, yss_ref, gp_ref, gs_ref, wp_hbm, ws_hbm, o_ref,
                  wp_s, ws_s, stage, sem, *, tm, chunk):
    i = pl.program_id(0)

    @pl.when(i == 0)
    def _():
        _load_weight(wp_hbm, wp_s, stage, sem, chunk)
        _load_weight(ws_hbm, ws_s, stage, sem, chunk)

    mp = jnp.dot(_rows2(ypp_ref, yps_ref, i, tm), wp_s[...], preferred_element_type=F32)
    ms = jnp.dot(_rows2(ysp_ref, yss_ref, i, tm), ws_s[...], preferred_element_type=F32)
    o_ref[...] = (gp_ref[...].astype(F32) * mp + gs_ref[...].astype(F32) * ms).astype(BF16)


def _merge(yp_p, yp_s, ys_p, ys_s, gates, w_bp, w_bs):
    tm, chunk = 256, 256
    return pl.pallas_call(
        functools.partial(_merge_kernel, tm=tm, chunk=chunk),
        grid=(N_ROWS // tm,),
        in_specs=[_spec_p(tm, D_MODEL), _spec_s(tm, D_MODEL), _spec_p(tm, D_INNER), _spec_s(tm, D_INNER),
                  pl.BlockSpec((tm, D_MODEL), lambda i: (i, 0)), pl.BlockSpec((tm, D_MODEL), lambda i: (i, 1)),
                  pl.BlockSpec(memory_space=pl.ANY), pl.BlockSpec(memory_space=pl.ANY)],
        out_specs=pl.BlockSpec((tm, D_MODEL), lambda i: (i, 0)),
        out_shape=jax.ShapeDtypeStruct((N_ROWS, D_MODEL), BF16),
        scratch_shapes=[pltpu.VMEM((D_MODEL, D_MODEL), BF16), pltpu.VMEM((D_INNER, D_MODEL), BF16),
                        pltpu.VMEM((2, chunk, D_MODEL), F32), pltpu.SemaphoreType.DMA((2,))],
        compiler_params=_params(1, 52),
        name="branch_merge",
    )(yp_p, yp_s, ys_p, ys_s, gates, gates, w_bp, w_bs)


def _pool_prompt_kernel(u_ref, uprev_ref, pw_ref, ps_ref, o_ref, stage):
    tiles_per_seq = SEQ // ROW_TILE
    lt = pl.program_id(0) % tiles_per_seq
    halo = POOL_BUF + 1
    stage[0:halo, :] = jnp.where(lt == 0, 0.0, uprev_ref[...])
    stage[halo:halo + ROW_TILE, :] = u_ref[...]
    t = lt * ROW_TILE + lax.broadcasted_iota(jnp.int32, (ROW_TILE, 1), 0)
    for g, w in enumerate(POOL_WINDOWS):
        lo = g * POOL_GROUP
        cur = u_ref[:, lo:lo + POOL_GROUP]
        s = cur
        for j in range(1, w):
            s = s + stage[halo - j:halo - j + ROW_TILE, lo:lo + POOL_GROUP]
        cnt = jnp.minimum(w, t + 1).astype(F32)
        pooled = (s / cnt - cur).astype(BF16)
        mixed = jnp.dot(pooled, pw_ref[g].astype(BF16), preferred_element_type=F32)
        o_ref[:, lo:lo + POOL_GROUP] = (mixed * ps_ref[:, lo:lo + POOL_GROUP]).astype(BF16)


def _pool_prompt(u_pool, pool_w, pool_scale):
    halo = POOL_BUF + 1
    per = ROW_TILE // halo
    return pl.pallas_call(
        _pool_prompt_kernel,
        grid=(N_PROMPT // ROW_TILE,),
        in_specs=[pl.BlockSpec((ROW_TILE, D_MODEL), lambda i: (i, 0)),
                  pl.BlockSpec((halo, D_MODEL), lambda i: (jnp.maximum(i * per - 1, 0), 0)),
                  pl.BlockSpec((len(POOL_WINDOWS), POOL_GROUP, POOL_GROUP), lambda i: (0, 0, 0)),
                  pl.BlockSpec((1, D_MODEL), lambda i: (0, 0))],
        out_specs=pl.BlockSpec((ROW_TILE, D_MODEL), lambda i: (i, 0)),
        out_shape=jax.ShapeDtypeStruct((N_PROMPT, D_MODEL), BF16),
        scratch_shapes=[pltpu.VMEM((halo + ROW_TILE, D_MODEL), F32)],
        compiler_params=_params(1, 48),
        name="pool_prompt",
    )(u_pool, u_pool, pool_w, pool_scale)


def _pool_sample_kernel(u_ref, buf_ref, pw_ref, ps_ref, o_ref):
    n_prev = min(PAST_LEN, POOL_BUF)
    for g, w in enumerate(POOL_WINDOWS):
        lo = g * POOL_GROUP

        def row(k):
            if k < POOL_BUF:
                return buf_ref[k, :, lo:lo + POOL_GROUP]
            return u_ref[k - POOL_BUF, :, lo:lo + POOL_GROUP]

        pooled = []
        for t in range(DEC_SEQ):
            cur = row(POOL_BUF + t)
            s = cur
            for j in range(1, w):
                s = s + row(POOL_BUF + t - j)
            cnt = float(min(w, t + 1 + n_prev))
            pooled.append(s / cnt - cur)
        pooled = jnp.concatenate(pooled, axis=0).astype(BF16)
        mixed = jnp.dot(pooled, pw_ref[g].astype(BF16), preferred_element_type=F32)
        y = (mixed * ps_ref[:, lo:lo + POOL_GROUP]).astype(BF16)
        for t in range(DEC_SEQ):
            o_ref[t, :, lo:lo + POOL_GROUP] = y[t * SAMPLE_BB:(t + 1) * SAMPLE_BB]


def _pool_sample(u_s, buf_t, pool_w, pool_scale):
    return pl.pallas_call(
        _pool_sample_kernel,
        grid=(DEC_BATCH // SAMPLE_BB,),
        in_specs=[pl.BlockSpec((DEC_SEQ, SAMPLE_BB, D_MODEL), lambda i: (0, i, 0)),
                  pl.BlockSpec((POOL_BUF, SAMPLE_BB, D_MODEL), lambda i: (0, i, 0)),
                  pl.BlockSpec((len(POOL_WINDOWS), POOL_GROUP, POOL_GROUP), lambda i: (0, 0, 0)),
                  pl.BlockSpec((1, D_MODEL), lambda i: (0, 0))],
        out_specs=pl.BlockSpec((DEC_SEQ, SAMPLE_BB, D_MODEL), lambda i: (0, i, 0)),
        out_shape=jax.ShapeDtypeStruct((DEC_SEQ, DEC_BATCH, D_MODEL), BF16),
        compiler_params=_params(1, 40),
        name="pool_sample",
    )(u_s, buf_t, pool_w, pool_scale)


def _gated_group_norm(y, x, zs, dskip, norm):
    y = (y + x * dskip) * zs.astype(F32)
    return (_rms(y) * norm).astype(BF16)


def _ssd_prompt_kernel(xbc_ref, prev_ref, zs_ref, dt_ref, dtb_ref, alog_ref, cw_ref, cb_ref, dskip_ref, norm_ref,
                       y_ref, hout_ref, h_s, xa_ref):
    c = pl.program_id(1)
    first = c == 0
    halo = CONV_HALO

    @pl.when(first)
    def _():
        h_s[...] = jnp.zeros_like(h_s)

    slab = 512
    for lo in range(0, CONV_DIM, slab):
        cols = slice(lo, lo + slab)
        ext = jnp.concatenate([jnp.where(first, 0.0, prev_ref[:, cols]), xbc_ref[:, cols]], axis=0)
        acc = ext * cw_ref[0:1, cols]
        for k in range(1, CONV_W):
            acc = pltpu.roll(acc, 1, axis=0) + ext * cw_ref[k:k + 1, cols]
        xa_ref[:, cols] = _silu(acc[halo:halo + CHUNK] + cb_ref[:, cols]).astype(BF16)

    dt_t = jax.nn.softplus(dt_ref[...] + dtb_ref[...])
    la_t = dt_t * (-jnp.exp(alog_ref[...]))
    lane_t = lax.broadcasted_iota(jnp.int32, la_t.shape, 1)
    s_t = la_t
    k = 1
    while k < CHUNK:
        s_t = s_t + jnp.where(lane_t >= k, pltpu.roll(s_t, k, axis=1), 0.0)
        k *= 2
    log2e = 1.0 / jnp.log(2.0)
    s2_t = s_t * log2e
    r2_t = (s_t - jnp.log(dt_t)) * log2e
    tail2_t = jnp.exp2(s2_t[:, CHUNK - 1:CHUNK] - r2_t)

    tri = (lax.broadcasted_iota(jnp.int32, (CHUNK, CHUNK), 0)
           >= lax.broadcasted_iota(jnp.int32, (CHUNK, CHUNK), 1))
    lo_half = lax.broadcasted_iota(jnp.int32, (CHUNK, LANES), 1) < HEAD_DIM

    for g in range(N_GROUPS):
        x = xa_ref[:, g * GROUP_CH:(g + 1) * GROUP_CH].astype(F32)
        bm_b = xa_ref[:, XBC_B0 + g * D_STATE:XBC_B0 + (g + 1) * D_STATE]
        cm_b = xa_ref[:, XBC_C0 + g * D_STATE:XBC_C0 + (g + 1) * D_STATE]
        cb = lax.dot_general(cm_b, bm_b, (((1,), (1,)), ((), ())), preferred_element_type=F32)
        bm_t = bm_b.astype(F32).T
        h_old = h_s[g]
        inter = jnp.dot(cm_b, h_old.astype(BF16), preferred_element_type=F32)

        ys, hs = [], []
        for i in range(PAIRS_PER_GROUP):
            w_pair, bt_pair, col_pair = [], [], []
            for j in (g * HEADS_PER_GROUP + 2 * i, g * HEADS_PER_GROUP + 2 * i + 1):
                colb = jnp.broadcast_to(s2_t[j:j + 1, :], (CHUNK, CHUNK)).T
                expo = jnp.where(tri, colb - r2_t[j:j + 1, :], -jnp.inf)
                w_pair.append((cb * jnp.exp2(expo)).astype(BF16))
                bt_pair.append((bm_t * tail2_t[j:j + 1, :]).astype(BF16))
                col_pair.append(colb)
            xp = x[:, i * LANES:(i + 1) * LANES]
            rhs = jnp.concatenate([jnp.where(lo_half, xp, 0.0), jnp.where(lo_half, 0.0, xp)], axis=0).astype(BF16)
            lhs = jnp.concatenate([jnp.concatenate(w_pair, axis=1), jnp.concatenate(bt_pair, axis=1)], axis=0)
            out = jnp.dot(lhs, rhs, preferred_element_type=F32)
            e_sel = jnp.exp2(jnp.where(lo_half, col_pair[0], col_pair[1]))
            ys.append(out[0:CHUNK] + inter[:, i * LANES:(i + 1) * LANES] * e_sel)
            hs.append(h_old[:, i * LANES:(i + 1) * LANES] * e_sel[CHUNK - 1:CHUNK, :] + out[CHUNK:2 * CHUNK])
        h_s[g] = jnp.concatenate(hs, axis=1)
        cols = slice(g * GROUP_CH, (g + 1) * GROUP_CH)
        y_ref[:, cols] = _gated_group_norm(jnp.concatenate(ys, axis=1), x, zs_ref[:, cols],
                                           dskip_ref[:, cols], norm_ref[:, cols])

    @pl.when(c == N_CHUNKS - 1)
    def _():
        for g in range(N_GROUPS):
            for i in range(PAIRS_PER_GROUP):
                pair = g * PAIRS_PER_GROUP + i
                h_pair = h_s[g, :, i * LANES:(i + 1) * LANES].T
                hout_ref[0, 0, 2 * pair:2 * pair + 2] = h_pair.reshape(2, HEAD_DIM, D_STATE)


def _ssd_prompt(xbc, zs, dt_t, dt_bias, a_log, conv_w, conv_b, dskip_e, ssm_norm):
    per = CHUNK // CONV_HALO

    def full(rows, width):
        return pl.BlockSpec((rows, width), lambda b, c: (0, 0))

    y, h = pl.pallas_call(
        _ssd_prompt_kernel,
        grid=(BATCH, N_CHUNKS),
        in_specs=[pl.BlockSpec((CHUNK, CONV_DIM), lambda b, c: (b * N_CHUNKS + c, 0)),
                  pl.BlockSpec((CONV_HALO, CONV_DIM),
                               lambda b, c: (jnp.maximum((b * N_CHUNKS + c) * per - 1, 0), 0)),
                  pl.BlockSpec((CHUNK, D_INNER), lambda b, c: (b * N_CHUNKS + c, 0)),
                  pl.BlockSpec((N_HEADS, CHUNK), lambda b, c: (0, b * N_CHUNKS + c)),
                  full(N_HEADS, 1), full(N_HEADS, 1),
                  full(CONV_W, CONV_DIM), full(1, CONV_DIM), full(1, D_INNER), full(1, D_INNER)],
        out_specs=[pl.BlockSpec((CHUNK, D_INNER), lambda b, c: (b * N_CHUNKS + c, 0)),
                   pl.BlockSpec((1, 1, N_HEADS, HEAD_DIM, D_STATE), lambda b, c: (0, b, 0, 0, 0))],
        out_shape=[jax.ShapeDtypeStruct((N_PROMPT, D_INNER), BF16),
                   jax.ShapeDtypeStruct((1, BATCH, N_HEADS, HEAD_DIM, D_STATE), F32)],
        scratch_shapes=[pltpu.VMEM((N_GROUPS, D_STATE, GROUP_CH), F32), pltpu.VMEM((CHUNK, CONV_DIM), BF16)],
        compiler_params=_params(2, 40),
        name="ssd_prompt",
    )(xbc, xbc, zs, dt_t, dt_bias, a_log, conv_w, conv_b, dskip_e, ssm_norm)
    return y, h


def _split3(v):
    hi = v.astype(BF16)
    r1 = v - hi.astype(F32)
    mid = r1.astype(BF16)
    lo = (r1 - mid.astype(F32)).astype(BF16)
    return hi, mid, lo


def _ssd_sample_kernel(xh_ref, xst_ref, bm_ref, bst_ref, cm_ref, cst_ref, zs_ref, dt_ref, dtb_ref, alog_ref,
                       cwx_ref, cbx_ref, cwb_ref, cbb_ref, cwc_ref, cbc_ref, dskip_ref, norm_ref, h0_ref,
                       y_ref, hout_ref):
    g = pl.program_id(1)
    bb = SAMPLE_BB
    rows = DEC_SEQ * bb

    def conv_silu(cur_ref, st_ref, w_ref, b_ref):
        full = [st_ref[k] for k in range(CONV_W - 1)] + [cur_ref[t] for t in range(DEC_SEQ)]
        outs = []
        for t in range(DEC_SEQ):
            acc = b_ref[...] + full[t] * w_ref[0:1, :]
            for k in range(1, CONV_W):
                acc = acc + full[t + k] * w_ref[k:k + 1, :]
            outs.append(_silu(acc))
        return outs

    x = conv_silu(xh_ref, xst_ref, cwx_ref, cbx_ref)
    bm = conv_silu(bm_ref, bst_ref, cwb_ref, cbb_ref)
    cm = conv_silu(cm_ref, cst_ref, cwc_ref, cbc_ref)

    a = -jnp.exp(alog_ref[...])
    dt = [jax.nn.softplus(dt_ref[t] + dtb_ref[...]) for t in range(DEC_SEQ)]
    s = [dt[0] * a]
    for t in range(1, DEC_SEQ):
        s.append(s[t - 1] + dt[t] * a)
    sel = (lax.broadcasted_iota(jnp.int32, (N_HEADS, GROUP_CH), 0)
           == g * HEADS_PER_GROUP + lax.broadcasted_iota(jnp.int32, (N_HEADS, GROUP_CH), 1) // HEAD_DIM)
    sel = jnp.where(sel, 1.0, 0.0).astype(BF16)

    def expand(v):
        return sum(jnp.dot(p, sel, preferred_element_type=F32) for p in _split3(v))

    dt_e = expand(jnp.concatenate(dt, axis=0))
    s_e = expand(jnp.concatenate(s, axis=0))
    dt_e = [dt_e[t * bb:(t + 1) * bb] for t in range(DEC_SEQ)]
    s_e = [s_e[t * bb:(t + 1) * bb] for t in range(DEC_SEQ)]

    c_all = jnp.concatenate(cm, axis=0).astype(BF16)
    b_all = jnp.concatenate(bm, axis=0)
    last = DEC_SEQ - 1
    xw_all = jnp.concatenate([x[t] * dt_e[t] * jnp.exp(s_e[last] - s_e[t]) for t in range(DEC_SEQ)], axis=0)
    xw_t = xw_all.T.astype(BF16)
    dec_t = jnp.concatenate([jnp.exp(s_e[last])] + [jnp.zeros((bb, GROUP_CH), F32)] * last, axis=0).T
    row_b = lax.broadcasted_iota(jnp.int32, (rows, 1), 0) % bb
    inter = jnp.zeros((rows, GROUP_CH), F32)
    for b in range(bb):
        mine = row_b == b
        h0 = h0_ref[0, b].reshape(GROUP_CH, D_STATE)
        yb = lax.dot_general(c_all, h0.astype(BF16), (((1,), (1,)), ((), ())), preferred_element_type=F32)
        inter = inter + jnp.where(mine, yb, 0.0)
        dh = jnp.dot(xw_t, jnp.where(mine, b_all, 0.0).astype(BF16), preferred_element_type=F32)
        h_new = h0 * jnp.broadcast_to(dec_t[:, b:b + 1], (GROUP_CH, D_STATE)) + dh
        hout_ref[0, b] = h_new.reshape(HEADS_PER_GROUP, HEAD_DIM, D_STATE)

    for t in range(DEC_SEQ):
        y = inter[t * bb:(t + 1) * bb] * jnp.exp(s_e[t])
        for u in range(t + 1):
            cb = jnp.sum(cm[t] * bm[u], axis=-1, keepdims=True)
            y = y + cb * jnp.exp(s_e[t] - s_e[u]) * dt_e[u] * x[u]
        y_ref[t] = _gated_group_norm(y, x[t], zs_ref[t], dskip_ref[...], norm_ref[...])


def _ssd_sample(xbc_s, conv_st, zs_s, dt_s, dt_bias_row, a_log_row, conv_w, conv_b, dskip_e, ssm_norm, h0):
    bb = SAMPLE_BB

    def cur(width, col0):
        return pl.BlockSpec((DEC_SEQ, bb, width), lambda i, g: (0, i, col0 + g))

    def st(width, col0):
        return pl.BlockSpec((CONV_W - 1, bb, width), lambda i, g: (0, i, col0 + g))

    def par(rows, width, col0):
        return pl.BlockSpec((rows, width), lambda i, g: (0, col0 + g))

    head_row = pl.BlockSpec((1, N_HEADS), lambda i, g: (0, 0))
    state = pl.BlockSpec((1, bb, HEADS_PER_GROUP, HEAD_DIM, D_STATE), lambda i, g: (0, i, g, 0, 0))
    y, h = pl.pallas_call(
        _ssd_sample_kernel,
        grid=(DEC_BATCH // bb, N_GROUPS),
        in_specs=[cur(GROUP_CH, 0), st(GROUP_CH, 0),
                  cur(D_STATE, XBC_B_COL), st(D_STATE, XBC_B_COL),
                  cur(D_STATE, XBC_C_COL), st(D_STATE, XBC_C_COL),
                  cur(GROUP_CH, 0),
                  pl.BlockSpec((DEC_SEQ, bb, N_HEADS), lambda i, g: (0, i, 0)),
                  head_row, head_row,
                  par(CONV_W, GROUP_CH, 0), par(1, GROUP_CH, 0),
                  par(CONV_W, D_STATE, XBC_B_COL), par(1, D_STATE, XBC_B_COL),
                  par(CONV_W, D_STATE, XBC_C_COL), par(1, D_STATE, XBC_C_COL),
                  par(1, GROUP_CH, 0), par(1, GROUP_CH, 0), state],
        out_specs=[pl.BlockSpec((DEC_SEQ, bb, GROUP_CH), lambda i, g: (0, i, g)), state],
        out_shape=[jax.ShapeDtypeStruct((DEC_SEQ, DEC_BATCH, D_INNER), BF16),
                   jax.ShapeDtypeStruct((1, DEC_BATCH, N_HEADS, HEAD_DIM, D_STATE), F32)],
        compiler_params=_params(2, 56),
        name="ssd_sample",
    )(xbc_s, conv_st, xbc_s, conv_st, xbc_s, conv_st, zs_s, dt_s, dt_bias_row, a_log_row,
      conv_w, conv_b, conv_w, conv_b, conv_w, conv_b, dskip_e, ssm_norm, h0)
    return y, h


def kernel(x_prompt, x_sample, c_prompt, c_sample, state_ssm, state_conv, state_pool, w_ada, b_ada, norm_ffn1,
           w13_ffn1, w2_ffn1, norm_mix, w_in, pool_w, pool_scale, conv_w, conv_b, dt_bias, a_log, d_skip,
           ssm_norm, w_branch_pool, w_branch_ssm, w_out, norm_ffn2, w13_ffn2, w2_ffn2, norm_final):
    d = D_MODEL

    def layer0(w):
        return w.reshape(w.shape[1:])

    def row(v):
        return v.reshape(1, -1)

    xp = x_prompt.reshape(N_PROMPT, d)
    xs = x_sample.transpose(1, 0, 2).reshape(N_SAMPLE, d)

    n_c = BATCH + DEC_BATCH
    c_all = jnp.pad(jnp.concatenate([c_prompt, c_sample], axis=0), ((0, -n_c % 8), (0, 0)))
    mods = _ada(c_all, layer0(w_ada), b_ada.reshape(1, N_MOD * d))
    mods_p = mods[:BATCH]
    mods_s = mods[BATCH:n_c]

    a1 = _normmod(xp, xs, row(norm_ffn1), mods_p, mods_s, 0, 1)
    act1 = _ffn_up(a1, layer0(w13_ffn1))
    h1, u = _res_block(act1, layer0(w2_ffn1), (xp, xs), mods_p, mods_s, 2, 0.5, row(norm_mix), (3, 4),
                       tm=256, chunk=176, vmem_mib=58)

    w_in_t = layer0(w_in).T
    col_z, col_xbc, col_dt = d, d + D_INNER, d + D_INNER + CONV_DIM
    conv_buf = layer0(state_conv)
    u_pool = _in_proj(u, w_in_t, 0, d, 1024, F32)
    zs = _in_proj(u, w_in_t, col_z, D_INNER, 1024, BF16, act=_silu)
    conv_w0 = layer0(conv_w)
    conv_b0 = row(conv_b)
    xbc = _in_proj(u, w_in_t, col_xbc, CONV_DIM, 1024, F32)
    dt_raw = _in_proj(u, w_in_t, col_dt, N_HEADS, N_HEADS, F32)
    gates = _in_proj(u, w_in_t, col_dt + N_HEADS, 2 * d, 1024, BF16, act=jax.nn.sigmoid)

    pw = layer0(pool_w)
    ps = row(pool_scale)
    u_pool_s = u_pool[N_PROMPT:].reshape(DEC_SEQ, DEC_BATCH, d)
    pool_buf = layer0(state_pool)
    y_pool_p = _pool_prompt(u_pool, pw, ps)
    y_pool_s = _pool_sample(u_pool_s, pool_buf.transpose(1, 0, 2), pw, ps).reshape(N_SAMPLE, d)

    dskip_e = row(jnp.repeat(d_skip.reshape(N_HEADS), HEAD_DIM))
    norm_row = row(ssm_norm)
    y_ssm_p, prompt_ssm = _ssd_prompt(xbc, zs, dt_raw[:N_PROMPT].T, dt_bias.reshape(N_HEADS, 1),
                                      a_log.reshape(N_HEADS, 1), conv_w0, conv_b0, dskip_e, norm_row)
    xbc_s = xbc[N_PROMPT:].reshape(DEC_SEQ, DEC_BATCH, CONV_DIM)
    y_ssm_s, sample_ssm = _ssd_sample(
        xbc_s, conv_buf.transpose(1, 0, 2), zs[N_PROMPT:].reshape(DEC_SEQ, DEC_BATCH, D_INNER),
        dt_raw[N_PROMPT:].reshape(DEC_SEQ, DEC_BATCH, N_HEADS), row(dt_bias), row(a_log),
        conv_w0, conv_b0, dskip_e, norm_row, state_ssm)

    merged = _merge(y_pool_p, y_pool_s, y_ssm_p, y_ssm_s.reshape(N_SAMPLE, D_INNER), gates,
                    layer0(w_branch_pool), layer0(w_branch_ssm))
    h2, a3 = _res_block(merged, layer0(w_out), h1, mods_p, mods_s, 5, 1.0, row(norm_ffn2), (6, 7),
                        tm=512, chunk=256, vmem_mib=52)
    act2 = _ffn_up(a3, layer0(w13_ffn2))
    y_p, y_s = _res_block(act2, layer0(w2_ffn2), h2, mods_p, mods_s, 8, 0.5, row(norm_final), None,
                          tm=256, chunk=176, vmem_mib=56)

    y_prompt = y_p.reshape(BATCH, SEQ, d)
    y_sample = y_s.reshape(DEC_SEQ, DEC_BATCH, d).transpose(1, 0, 2)
    keep = CONV_W - 1
    seq_end = (jnp.arange(BATCH) + 1) * SEQ
    conv_rows = (seq_end[:, None] - keep + jnp.arange(keep)[None, :]).reshape(-1)
    pool_rows = (seq_end[:, None] - POOL_BUF + jnp.arange(POOL_BUF)[None, :]).reshape(-1)
    prompt_conv = jnp.take(xbc, conv_rows, axis=0).reshape(1, BATCH, keep, CONV_DIM)
    prompt_pool = jnp.take(u_pool, pool_rows, axis=0).reshape(1, BATCH, POOL_BUF, d)
    sample_conv = jnp.concatenate([conv_buf, xbc_s.transpose(1, 0, 2)], axis=1)[:, -keep:][None]
    sample_pool = jnp.concatenate([pool_buf, u_pool_s.transpose(1, 0, 2)], axis=1)[:, -POOL_BUF:][None]
    return (y_prompt, y_sample, prompt_ssm, prompt_conv, prompt_pool, sample_ssm, sample_conv, sample_pool)
```

```python
import functools

import jax
import jax.numpy as jnp
from jax import lax
from jax.experimental import pallas as pl
from jax.experimental.pallas import tpu as pltpu

F32 = jnp.float32
BF16 = jnp.bfloat16

D_MODEL = 2048
BATCH = 4
SEQ = 2048
DEC_BATCH = 128
DEC_SEQ = 4
PAST_LEN = 16384
POOL_WINDOWS = (2, 4, 8, 16)
POOL_GROUP = D_MODEL // len(POOL_WINDOWS)
POOL_BUF = max(POOL_WINDOWS) - 1
D_INNER = 2 * D_MODEL
HEAD_DIM = 64
N_HEADS = D_INNER // HEAD_DIM
D_STATE = 128
N_GROUPS = 8
HEADS_PER_GROUP = N_HEADS // N_GROUPS
GROUP_CH = D_INNER // N_GROUPS
CONV_W = 4
CONV_DIM = D_INNER + 2 * N_GROUPS * D_STATE
CHUNK = 128
D_FF = 256 * ((8 * D_MODEL // 3 + 255) // 256)
N_MOD = 9
EPS = 1e-6

N_PROMPT = BATCH * SEQ
N_SAMPLE = DEC_BATCH * DEC_SEQ
N_ROWS = N_PROMPT + N_SAMPLE
ROW_TILE = 512
N_ROW_TILES = N_ROWS // ROW_TILE
BIG_TILE = 1024
FULL_BIG_TILES = N_ROWS // BIG_TILE
REM_ROWS = N_ROWS - FULL_BIG_TILES * BIG_TILE
N_BIG_TILES = FULL_BIG_TILES + (1 if REM_ROWS else 0)
N_CHUNKS = SEQ // CHUNK
LANES = 128
PAIRS_PER_GROUP = HEADS_PER_GROUP // 2
SAMPLE_BB = 32
XBC_B0 = D_INNER
XBC_C0 = D_INNER + N_GROUPS * D_STATE
XBC_B_COL = XBC_B0 // D_STATE
XBC_C_COL = XBC_C0 // D_STATE
CONV_HALO = 8
MIB = 1024 * 1024


def _params(n_axes, vmem_mib):
    return pltpu.CompilerParams(dimension_semantics=("arbitrary",) * n_axes,
                                vmem_limit_bytes=vmem_mib * MIB)


def _silu(x):
    return x * jax.nn.sigmoid(x)


def _row_mods(mp_ref, ms_ref, tile, tm):
    seq = jnp.minimum(tile // (SEQ // tm), BATCH - 1)
    ms = ms_ref[...]
    ms = jnp.concatenate([ms] * (tm // DEC_BATCH), axis=0)
    return jnp.where(tile >= N_PROMPT // tm, ms, mp_ref[pl.ds(seq, 1), :])


def _rows2(xp_ref, xs_ref, tile, tm):
    return jnp.where(tile >= N_PROMPT // tm, xs_ref[...], xp_ref[...])


def _spec_p(tm, width, col=0):
    last = N_PROMPT // tm - 1
    return pl.BlockSpec((tm, width), lambda i: (jnp.minimum(i, last), col))


def _spec_s(tm, width, col=0):
    first = N_PROMPT // tm
    return pl.BlockSpec((tm, width), lambda i: (jnp.maximum(i - first, 0), col))


def _mod_specs(chunk):
    return [pl.BlockSpec((BATCH, D_MODEL), lambda i: (0, chunk)),
            pl.BlockSpec((DEC_BATCH, D_MODEL), lambda i: (0, chunk))]


def _rms(x):
    return x * lax.rsqrt(jnp.mean(x * x, axis=-1, keepdims=True) + EPS)


def _ada_kernel(c_ref, w_ref, b_ref, o_ref):
    a = _silu(c_ref[...]).astype(BF16)
    o_ref[...] = jnp.dot(a, w_ref[...].astype(BF16), preferred_element_type=F32) + b_ref[...]


def _ada(c_all, w, b):
    m = c_all.shape[0]
    n = w.shape[1]
    tn = 1024
    return pl.pallas_call(
        _ada_kernel,
        grid=(n // tn,),
        in_specs=[pl.BlockSpec((m, D_MODEL), lambda j: (0, 0)),
                  pl.BlockSpec((D_MODEL, tn), lambda j: (0, j)),
                  pl.BlockSpec((1, tn), lambda j: (0, j))],
        out_specs=pl.BlockSpec((m, tn), lambda j: (0, j)),
        out_shape=jax.ShapeDtypeStruct((m, n), F32),
        compiler_params=_params(1, 40),
        name="ada_mods",
    )(c_all, w, b)


def _normmod_kernel(xp_ref, xs_ref, g_ref, shp_ref, shs_ref, scp_ref, scs_ref, o_ref):
    i = pl.program_id(0)
    y = _rms(_rows2(xp_ref, xs_ref, i, ROW_TILE)) * g_ref[...]
    shift = _row_mods(shp_ref, shs_ref, i, ROW_TILE)
    scale = _row_mods(scp_ref, scs_ref, i, ROW_TILE)
    o_ref[...] = (y * (1.0 + scale) + shift).astype(BF16)


def _normmod(xp, xs, g, mods_p, mods_s, shift_chunk, scale_chunk):
    return pl.pallas_call(
        _normmod_kernel,
        grid=(N_ROW_TILES,),
        in_specs=[_spec_p(ROW_TILE, D_MODEL), _spec_s(ROW_TILE, D_MODEL),
                  pl.BlockSpec((1, D_MODEL), lambda i: (0, 0))]
                 + _mod_specs(shift_chunk) + _mod_specs(scale_chunk),
        out_specs=pl.BlockSpec((ROW_TILE, D_MODEL), lambda i: (i, 0)),
        out_shape=jax.ShapeDtypeStruct((N_ROWS, D_MODEL), BF16),
        compiler_params=_params(1, 48),
        name="norm_modulate",
    )(xp, xs, g, mods_p, mods_s, mods_p, mods_s)


def _per_row_tile(i, body):
    @pl.when(i < FULL_BIG_TILES)
    def _():
        body(BIG_TILE)

    @pl.when(i == FULL_BIG_TILES)
    def _():
        body(REM_ROWS)


def _ffn_up_kernel(a_ref, wa_ref, wb_ref, o_ref, wa_s, wb_s):
    i = pl.program_id(1)

    @pl.when(i == 0)
    def _():
        wa_s[...] = wa_ref[...].astype(BF16)
        wb_s[...] = wb_ref[...].astype(BF16)

    def body(rows):
        a = a_ref[0:rows, :]
        ha = jnp.dot(a, wa_s[...], preferred_element_type=F32)
        hb = jnp.dot(a, wb_s[...], preferred_element_type=F32)
        o_ref[0:rows, :] = (_silu(ha) * hb).astype(BF16)

    _per_row_tile(i, body)


def _ffn_up(a, w13):
    tn = 512
    nt = D_FF // tn
    return pl.pallas_call(
        _ffn_up_kernel,
        grid=(nt, N_BIG_TILES),
        in_specs=[pl.BlockSpec((BIG_TILE, D_MODEL), lambda j, i: (i, 0)),
                  pl.BlockSpec((D_MODEL, tn), lambda j, i: (0, j)),
                  pl.BlockSpec((D_MODEL, tn), lambda j, i: (0, nt + j))],
        out_specs=pl.BlockSpec((BIG_TILE, tn), lambda j, i: (i, j)),
        out_shape=jax.ShapeDtypeStruct((N_ROWS, D_FF), BF16),
        scratch_shapes=[pltpu.VMEM((D_MODEL, tn), BF16), pltpu.VMEM((D_MODEL, tn), BF16)],
        compiler_params=_params(2, 56),
        name="ffn_up",
    )(a, w13, w13)


def _in_proj_kernel(a_ref, wt_ref, o_ref, wt_s, *, act):
    i = pl.program_id(1)

    @pl.when(i == 0)
    def _():
        wt_s[...] = wt_ref[...].astype(BF16)

    def body(rows):
        acc = lax.dot_general(a_ref[0:rows, :], wt_s[...], (((1,), (1,)), ((), ())), preferred_element_type=F32)
        if act is not None:
            acc = act(acc)
        o_ref[0:rows, :] = acc.astype(o_ref.dtype)

    _per_row_tile(i, body)


def _in_proj(a, wt, row0, nrows, tn, out_dtype, act=None):
    k = a.shape[1]
    if row0 % tn == 0:
        w_spec = pl.BlockSpec((tn, k), lambda j, i: (row0 // tn + j, 0))
    else:
        assert row0 % 8 == 0
        w_spec = pl.BlockSpec((pl.Element(tn), pl.Element(k)),
                              lambda j, i: (pl.multiple_of(row0 + j * tn, 8), 0))
    return pl.pallas_call(
        functools.partial(_in_proj_kernel, act=act),
        grid=(nrows // tn, N_BIG_TILES),
        in_specs=[pl.BlockSpec((BIG_TILE, k), lambda j, i: (i, 0)), w_spec],
        out_specs=pl.BlockSpec((BIG_TILE, tn), lambda j, i: (i, j)),
        out_shape=jax.ShapeDtypeStruct((N_ROWS, nrows), out_dtype),
        scratch_shapes=[pltpu.VMEM((tn, k), BF16)],
        compiler_params=_params(2, 56),
        name="in_proj",
    )(a, wt)


def _load_weight(w_hbm, w_s, stage, sem, chunk):
    n = w_hbm.shape[0] // chunk

    def copy(c):
        return pltpu.make_async_copy(w_hbm.at[pl.ds(c * chunk, chunk), :], stage.at[c % 2], sem.at[c % 2])

    copy(0).start()
    for c in range(n):
        if c + 1 < n:
            copy(c + 1).start()
        copy(c).wait()
        w_s[c * chunk:(c + 1) * chunk, :] = stage[c % 2].astype(BF16)


def _res_block_kernel(*refs, tm, scale, two_source_res, final, chunk):
    refs = list(refs)
    a_ref, w_hbm = refs[:2]
    pos = 2
    if two_source_res:
        resp_ref, ress_ref = refs[pos:pos + 2]
        pos += 2
    else:
        res_ref = refs[pos]
        pos += 1
    gp_ref, gs_ref, g_ref = refs[pos:pos + 3]
    pos += 3
    if not final:
        shp_ref, shs_ref, scp_ref, scs_ref = refs[pos:pos + 4]
        pos += 4
    outs = refs[pos:pos + 2]
    w_s, stage, sem = refs[pos + 2:]
    i = pl.program_id(0)

    @pl.when(i == 0)
    def _():
        _load_weight(w_hbm, w_s, stage, sem, chunk)

    res = _rows2(resp_ref, ress_ref, i, tm) if two_source_res else res_ref[...]
    gate = _row_mods(gp_ref, gs_ref, i, tm)
    if scale != 1.0:
        gate = scale * gate
    h = res + gate * jnp.dot(a_ref[...], w_s[...], preferred_element_type=F32)
    y = _rms(h) * g_ref[...]
    if final:
        yp_ref, ys_ref = outs

        @pl.when(i < N_PROMPT // tm)
        def _():
            yp_ref[...] = y

        @pl.when(i >= N_PROMPT // tm)
        def _():
            ys_ref[...] = y
    else:
        h_ref, nxt_ref = outs
        h_ref[...] = h
        shift = _row_mods(shp_ref, shs_ref, i, tm)
        sc = _row_mods(scp_ref, scs_ref, i, tm)
        nxt_ref[...] = (y * (1.0 + sc) + shift).astype(BF16)


def _res_block(a, w, res, mods_p, mods_s, gate_chunk, scale, norm_g, next_chunks, tm, chunk, vmem_mib):
    k = a.shape[1]
    final = next_chunks is None
    two = isinstance(res, tuple)
    row = pl.BlockSpec((tm, D_MODEL), lambda i: (i, 0))
    in_specs = [pl.BlockSpec((tm, k), lambda i: (i, 0)), pl.BlockSpec(memory_space=pl.ANY)]
    args = [a, w]
    if two:
        in_specs += [_spec_p(tm, D_MODEL), _spec_s(tm, D_MODEL)]
        args += list(res)
    else:
        in_specs.append(row)
        args.append(res)
    in_specs += _mod_specs(gate_chunk) + [pl.BlockSpec((1, D_MODEL), lambda i: (0, 0))]
    args += [mods_p, mods_s, norm_g]
    if final:
        out_specs = [_spec_p(tm, D_MODEL), _spec_s(tm, D_MODEL)]
        out_shape = [jax.ShapeDtypeStruct((N_PROMPT, D_MODEL), F32), jax.ShapeDtypeStruct((N_SAMPLE, D_MODEL), F32)]
    else:
        in_specs += _mod_specs(next_chunks[0]) + _mod_specs(next_chunks[1])
        args += [mods_p, mods_s, mods_p, mods_s]
        out_specs = [row, row]
        out_shape = [jax.ShapeDtypeStruct((N_ROWS, D_MODEL), F32), jax.ShapeDtypeStruct((N_ROWS, D_MODEL), BF16)]
    return pl.pallas_call(
        functools.partial(_res_block_kernel, tm=tm, scale=scale, two_source_res=two, final=final, chunk=chunk),
        grid=(N_ROWS // tm,),
        in_specs=in_specs,
        out_specs=out_specs,
        out_shape=out_shape,
        scratch_shapes=[pltpu.VMEM((k, D_MODEL), BF16), pltpu.VMEM((2, chunk, D_MODEL), F32),
                        pltpu.SemaphoreType.DMA((2,))],
        compiler_params=_params(1, vmem_mib),
        name="res_block_final" if final else "res_block",
    )(*args)


def _merge_kernel(ypp_ref, yps_ref, ysp_ref, yss_ref, gp_ref, gs_ref, wp_hbm, ws_hbm, o_ref,
                  wp_s, ws_s, stage, sem, *, tm, chunk):
    i = pl.program_id(0)

    @pl.when(i == 0)
    def _():
        _load_weight(wp_hbm, wp_s, stage, sem, chunk)
        _load_weight(ws_hbm, ws_s, stage, sem, chunk)

    mp = jnp.dot(_rows2(ypp_ref, yps_ref, i, tm), wp_s[...], preferred_element_type=F32)
    ms = jnp.dot(_rows2(ysp_ref, yss_ref, i, tm), ws_s[...], preferred_element_type=F32)
    o_ref[...] = (gp_ref[...].astype(F32) * mp + gs_ref[...].astype(F32) * ms).astype(BF16)


def _merge(yp_p, yp_s, ys_p, ys_s, gates, w_bp, w_bs):
    tm, chunk = 256, 256
    return pl.pallas_call(
        functools.partial(_merge_kernel, tm=tm, chunk=chunk),
        grid=(N_ROWS // tm,),
        in_specs=[_spec_p(tm, D_MODEL), _spec_s(tm, D_MODEL), _spec_p(tm, D_INNER), _spec_s(tm, D_INNER),
                  pl.BlockSpec((tm, D_MODEL), lambda i: (i, 0)), pl.BlockSpec((tm, D_MODEL), lambda i: (i, 1)),
                  pl.BlockSpec(memory_space=pl.ANY), pl.BlockSpec(memory_space=pl.ANY)],
        out_specs=pl.BlockSpec((tm, D_MODEL), lambda i: (i, 0)),
        out_shape=jax.ShapeDtypeStruct((N_ROWS, D_MODEL), BF16),
        scratch_shapes=[pltpu.VMEM((D_MODEL, D_MODEL), BF16), pltpu.VMEM((D_INNER, D_MODEL), BF16),
                        pltpu.VMEM((2, chunk, D_MODEL), F32), pltpu.SemaphoreType.DMA((2,))],
        compiler_params=_params(1, 52),
        name="branch_merge",
    )(yp_p, yp_s, ys_p, ys_s, gates, gates, w_bp, w_bs)


def _pool_prompt_kernel(u_ref, uprev_ref, pw_ref, ps_ref, o_ref, stage):
    tiles_per_seq = SEQ // ROW_TILE
    lt = pl.program_id(0) % tiles_per_seq
    halo = POOL_BUF + 1
    stage[0:halo, :] = jnp.where(lt == 0, 0.0, uprev_ref[...])
    stage[halo:halo + ROW_TILE, :] = u_ref[...]
    t = lt * ROW_TILE + lax.broadcasted_iota(jnp.int32, (ROW_TILE, 1), 0)
    for g, w in enumerate(POOL_WINDOWS):
        lo = g * POOL_GROUP
        cur = u_ref[:, lo:lo + POOL_GROUP]
        s = cur
        for j in range(1, w):
            s = s + stage[halo - j:halo - j + ROW_TILE, lo:lo + POOL_GROUP]
        cnt = jnp.minimum(w, t + 1).astype(F32)
        pooled = (s / cnt - cur).astype(BF16)
        mixed = jnp.dot(pooled, pw_ref[g].astype(BF16), preferred_element_type=F32)
        o_ref[:, lo:lo + POOL_GROUP] = (mixed * ps_ref[:, lo:lo + POOL_GROUP]).astype(BF16)


def _pool_prompt(u_pool, pool_w, pool_scale):
    halo = POOL_BUF + 1
    per = ROW_TILE // halo
    return pl.pallas_call(
        _pool_prompt_kernel,
        grid=(N_PROMPT // ROW_TILE,),
        in_specs=[pl.BlockSpec((ROW_TILE, D_MODEL), lambda i: (i, 0)),
                  pl.BlockSpec((halo, D_MODEL), lambda i: (jnp.maximum(i * per - 1, 0), 0)),
                  pl.BlockSpec((len(POOL_WINDOWS), POOL_GROUP, POOL_GROUP), lambda i: (0, 0, 0)),
                  pl.BlockSpec((1, D_MODEL), lambda i: (0, 0))],
        out_specs=pl.BlockSpec((ROW_TILE, D_MODEL), lambda i: (i, 0)),
        out_shape=jax.ShapeDtypeStruct((N_PROMPT, D_MODEL), BF16),
        scratch_shapes=[pltpu.VMEM((halo + ROW_TILE, D_MODEL), F32)],
        compiler_params=_params(1, 48),
        name="pool_prompt",
    )(u_pool, u_pool, pool_w, pool_scale)


def _pool_sample_kernel(u_ref, buf_ref, pw_ref, ps_ref, o_ref):
    n_prev = min(PAST_LEN, POOL_BUF)
    for g, w in enumerate(POOL_WINDOWS):
        lo = g * POOL_GROUP

        def row(k):
            if k < POOL_BUF:
                return buf_ref[k, :, lo:lo + POOL_GROUP]
            return u_ref[k - POOL_BUF, :, lo:lo + POOL_GROUP]

        pooled = []
        for t in range(DEC_SEQ):
            cur = row(POOL_BUF + t)
            s = cur
            for j in range(1, w):
                s = s + row(POOL_BUF + t - j)
            cnt = float(min(w, t + 1 + n_prev))
            pooled.append(s / cnt - cur)
        pooled = jnp.concatenate(pooled, axis=0).astype(BF16)
        mixed = jnp.dot(pooled, pw_ref[g].astype(BF16), preferred_element_type=F32)
        y = (mixed * ps_ref[:, lo:lo + POOL_GROUP]).astype(BF16)
        for t in range(DEC_SEQ):
            o_ref[t, :, lo:lo + POOL_GROUP] = y[t * SAMPLE_BB:(t + 1) * SAMPLE_BB]


def _pool_sample(u_s, buf_t, pool_w, pool_scale):
    return pl.pallas_call(
        _pool_sample_kernel,
        grid=(DEC_BATCH // SAMPLE_BB,),
        in_specs=[pl.BlockSpec((DEC_SEQ, SAMPLE_BB, D_MODEL), lambda i: (0, i, 0)),
                  pl.BlockSpec((POOL_BUF, SAMPLE_BB, D_MODEL), lambda i: (0, i, 0)),
                  pl.BlockSpec((len(POOL_WINDOWS), POOL_GROUP, POOL_GROUP), lambda i: (0, 0, 0)),
                  pl.BlockSpec((1, D_MODEL), lambda i: (0, 0))],
        out_specs=pl.BlockSpec((DEC_SEQ, SAMPLE_BB, D_MODEL), lambda i: (0, i, 0)),
        out_shape=jax.ShapeDtypeStruct((DEC_SEQ, DEC_BATCH, D_MODEL), BF16),
        compiler_params=_params(1, 40),
        name="pool_sample",
    )(u_s, buf_t, pool_w, pool_scale)


def _gated_group_norm(y, x, zs, dskip, norm):
    y = (y + x * dskip) * zs.astype(F32)
    return (_rms(y) * norm).astype(BF16)


def _ssd_prompt_kernel(xbc_ref, prev_ref, zs_ref, dt_ref, dtb_ref, alog_ref, cw_ref, cb_ref, dskip_ref, norm_ref,
                       y_ref, hout_ref, h_s, xa_ref):
    c = pl.program_id(1)
    first = c == 0
    halo = CONV_HALO

    @pl.when(first)
    def _():
        h_s[...] = jnp.zeros_like(h_s)

    slab = 512
    for lo in range(0, CONV_DIM, slab):
        cols = slice(lo, lo + slab)
        ext = jnp.concatenate([jnp.where(first, 0.0, prev_ref[:, cols]), xbc_ref[:, cols]], axis=0)
        acc = ext * cw_ref[0:1, cols]
        for k in range(1, CONV_W):
            acc = pltpu.roll(acc, 1, axis=0) + ext * cw_ref[k:k + 1, cols]
        xa_ref[:, cols] = _silu(acc[halo:halo + CHUNK] + cb_ref[:, cols]).astype(BF16)

    dt_t = jax.nn.softplus(dt_ref[...] + dtb_ref[...])
    la_t = dt_t * (-jnp.exp(alog_ref[...]))
    lane_t = lax.broadcasted_iota(jnp.int32, la_t.shape, 1)
    s_t = la_t
    k = 1
    while k < CHUNK:
        s_t = s_t + jnp.where(lane_t >= k, pltpu.roll(s_t, k, axis=1), 0.0)
        k *= 2
    log2e = 1.0 / jnp.log(2.0)
    s2_t = s_t * log2e
    r2_t = (s_t - jnp.log(dt_t)) * log2e
    tail2_t = jnp.exp2(s2_t[:, CHUNK - 1:CHUNK] - r2_t)

    tri = (lax.broadcasted_iota(jnp.int32, (CHUNK, CHUNK), 0)
           >= lax.broadcasted_iota(jnp.int32, (CHUNK, CHUNK), 1))
    lo_half = lax.broadcasted_iota(jnp.int32, (CHUNK, LANES), 1) < HEAD_DIM

    for g in range(N_GROUPS):
        x = xa_ref[:, g * GROUP_CH:(g + 1) * GROUP_CH].astype(F32)
        bm_b = xa_ref[:, XBC_B0 + g * D_STATE:XBC_B0 + (g + 1) * D_STATE]
        cm_b = xa_ref[:, XBC_C0 + g * D_STATE:XBC_C0 + (g + 1) * D_STATE]
        cb = lax.dot_general(cm_b, bm_b, (((1,), (1,)), ((), ())), preferred_element_type=F32)
        bm_t = bm_b.astype(F32).T
        h_old = h_s[g]
        inter = jnp.dot(cm_b, h_old.astype(BF16), preferred_element_type=F32)

        ys, hs = [], []
        for i in range(PAIRS_PER_GROUP):
            w_pair, bt_pair, col_pair = [], [], []
            for j in (g * HEADS_PER_GROUP + 2 * i, g * HEADS_PER_GROUP + 2 * i + 1):
                colb = jnp.broadcast_to(s2_t[j:j + 1, :], (CHUNK, CHUNK)).T
                expo = jnp.where(tri, colb - r2_t[j:j + 1, :], -jnp.inf)
                w_pair.append((cb * jnp.exp2(expo)).astype(BF16))
                bt_pair.append((bm_t * tail2_t[j:j + 1, :]).astype(BF16))
                col_pair.append(colb)
            xp = x[:, i * LANES:(i + 1) * LANES]
            rhs = jnp.concatenate([jnp.where(lo_half, xp, 0.0), jnp.where(lo_half, 0.0, xp)], axis=0).astype(BF16)
            lhs = jnp.concatenate([jnp.concatenate(w_pair, axis=1), jnp.concatenate(bt_pair, axis=1)], axis=0)
            out = jnp.dot(lhs, rhs, preferred_element_type=F32)
            e_sel = jnp.exp2(jnp.where(lo_half, col_pair[0], col_pair[1]))
            ys.append(out[0:CHUNK] + inter[:, i * LANES:(i + 1) * LANES] * e_sel)
            hs.append(h_old[:, i * LANES:(i + 1) * LANES] * e_sel[CHUNK - 1:CHUNK, :] + out[CHUNK:2 * CHUNK])
        h_s[g] = jnp.concatenate(hs, axis=1)
        cols = slice(g * GROUP_CH, (g + 1) * GROUP_CH)
        y_ref[:, cols] = _gated_group_norm(jnp.concatenate(ys, axis=1), x, zs_ref[:, cols],
                                           dskip_ref[:, cols], norm_ref[:, cols])

    @pl.when(c == N_CHUNKS - 1)
    def _():
        for g in range(N_GROUPS):
            for i in range(PAIRS_PER_GROUP):
                pair = g * PAIRS_PER_GROUP + i
                h_pair = h_s[g, :, i * LANES:(i + 1) * LANES].T
                hout_ref[0, 0, 2 * pair:2 * pair + 2] = h_pair.reshape(2, HEAD_DIM, D_STATE)


def _ssd_prompt(xbc, zs, dt_t, dt_bias, a_log, conv_w, conv_b, dskip_e, ssm_norm):
    per = CHUNK // CONV_HALO

    def full(rows, width):
        return pl.BlockSpec((rows, width), lambda b, c: (0, 0))

    y, h = pl.pallas_call(
        _ssd_prompt_kernel,
        grid=(BATCH, N_CHUNKS),
        in_specs=[pl.BlockSpec((CHUNK, CONV_DIM), lambda b, c: (b * N_CHUNKS + c, 0)),
                  pl.BlockSpec((CONV_HALO, CONV_DIM),
                               lambda b, c: (jnp.maximum((b * N_CHUNKS + c) * per - 1, 0), 0)),
                  pl.BlockSpec((CHUNK, D_INNER), lambda b, c: (b * N_CHUNKS + c, 0)),
                  pl.BlockSpec((N_HEADS, CHUNK), lambda b, c: (0, b * N_CHUNKS + c)),
                  full(N_HEADS, 1), full(N_HEADS, 1),
                  full(CONV_W, CONV_DIM), full(1, CONV_DIM), full(1, D_INNER), full(1, D_INNER)],
        out_specs=[pl.BlockSpec((CHUNK, D_INNER), lambda b, c: (b * N_CHUNKS + c, 0)),
                   pl.BlockSpec((1, 1, N_HEADS, HEAD_DIM, D_STATE), lambda b, c: (0, b, 0, 0, 0))],
        out_shape=[jax.ShapeDtypeStruct((N_PROMPT, D_INNER), BF16),
                   jax.ShapeDtypeStruct((1, BATCH, N_HEADS, HEAD_DIM, D_STATE), F32)],
        scratch_shapes=[pltpu.VMEM((N_GROUPS, D_STATE, GROUP_CH), F32), pltpu.VMEM((CHUNK, CONV_DIM), BF16)],
        compiler_params=_params(2, 40),
        name="ssd_prompt",
    )(xbc, xbc, zs, dt_t, dt_bias, a_log, conv_w, conv_b, dskip_e, ssm_norm)
    return y, h


def _split3(v):
    hi = v.astype(BF16)
    r1 = v - hi.astype(F32)
    mid = r1.astype(BF16)
    lo = (r1 - mid.astype(F32)).astype(BF16)
    return hi, mid, lo


def _ssd_sample_kernel(xh_ref, xst_ref, bm_ref, bst_ref, cm_ref, cst_ref, zs_ref, dt_ref, dtb_ref, alog_ref,
                       cwx_ref, cbx_ref, cwb_ref, cbb_ref, cwc_ref, cbc_ref, dskip_ref, norm_ref, h0_ref,
                       y_ref, hout_ref):
    g = pl.program_id(1)
    bb = SAMPLE_BB
    rows = DEC_SEQ * bb

    def conv_silu(cur_ref, st_ref, w_ref, b_ref):
        full = [st_ref[k] for k in range(CONV_W - 1)] + [cur_ref[t] for t in range(DEC_SEQ)]
        outs = []
        for t in range(DEC_SEQ):
            acc = b_ref[...] + full[t] * w_ref[0:1, :]
            for k in range(1, CONV_W):
                acc = acc + full[t + k] * w_ref[k:k + 1, :]
            outs.append(_silu(acc))
        return outs

    x = conv_silu(xh_ref, xst_ref, cwx_ref, cbx_ref)
    bm = conv_silu(bm_ref, bst_ref, cwb_ref, cbb_ref)
    cm = conv_silu(cm_ref, cst_ref, cwc_ref, cbc_ref)

    a = -jnp.exp(alog_ref[...])
    dt = [jax.nn.softplus(dt_ref[t] + dtb_ref[...]) for t in range(DEC_SEQ)]
    s = [dt[0] * a]
    for t in range(1, DEC_SEQ):
        s.append(s[t - 1] + dt[t] * a)
    sel = (lax.broadcasted_iota(jnp.int32, (N_HEADS, GROUP_CH), 0)
           == g * HEADS_PER_GROUP + lax.broadcasted_iota(jnp.int32, (N_HEADS, GROUP_CH), 1) // HEAD_DIM)
    sel = jnp.where(sel, 1.0, 0.0).astype(BF16)

    def expand(v):
        return sum(jnp.dot(p, sel, preferred_element_type=F32) for p in _split3(v))

    dt_e = expand(jnp.concatenate(dt, axis=0))
    s_e = expand(jnp.concatenate(s, axis=0))
    dt_e = [dt_e[t * bb:(t + 1) * bb] for t in range(DEC_SEQ)]
    s_e = [s_e[t * bb:(t + 1) * bb] for t in range(DEC_SEQ)]

    c_all = jnp.concatenate(cm, axis=0).astype(BF16)
    b_all = jnp.concatenate(bm, axis=0)
    last = DEC_SEQ - 1
    xw_all = jnp.concatenate([x[t] * dt_e[t] * jnp.exp(s_e[last] - s_e[t]) for t in range(DEC_SEQ)], axis=0)
    xw_t = xw_all.T.astype(BF16)
    dec_t = jnp.concatenate([jnp.exp(s_e[last])] + [jnp.zeros((bb, GROUP_CH), F32)] * last, axis=0).T
    row_b = lax.broadcasted_iota(jnp.int32, (rows, 1), 0) % bb
    inter = jnp.zeros((rows, GROUP_CH), F32)
    for b in range(bb):
        mine = row_b == b
        h0 = h0_ref[0, b].reshape(GROUP_CH, D_STATE)
        yb = lax.dot_general(c_all, h0.astype(BF16), (((1,), (1,)), ((), ())), preferred_element_type=F32)
        inter = inter + jnp.where(mine, yb, 0.0)
        dh = jnp.dot(xw_t, jnp.where(mine, b_all, 0.0).astype(BF16), preferred_element_type=F32)
        h_new = h0 * jnp.broadcast_to(dec_t[:, b:b + 1], (GROUP_CH, D_STATE)) + dh
        hout_ref[0, b] = h_new.reshape(HEADS_PER_GROUP, HEAD_DIM, D_STATE)

    for t in range(DEC_SEQ):
        y = inter[t * bb:(t + 1) * bb] * jnp.exp(s_e[t])
        for u in range(t + 1):
            cb = jnp.sum(cm[t] * bm[u], axis=-1, keepdims=True)
            y = y + cb * jnp.exp(s_e[t] - s_e[u]) * dt_e[u] * x[u]
        y_ref[t] = _gated_group_norm(y, x[t], zs_ref[t], dskip_ref[...], norm_ref[...])


def _ssd_sample(xbc_s, conv_st, zs_s, dt_s, dt_bias_row, a_log_row, conv_w, conv_b, dskip_e, ssm_norm, h0):
    bb = SAMPLE_BB

    def cur(width, col0):
        return pl.BlockSpec((DEC_SEQ, bb, width), lambda i, g: (0, i, col0 + g))

    def st(width, col0):
        return pl.BlockSpec((CONV_W - 1, bb, width), lambda i, g: (0, i, col0 + g))

    def par(rows, width, col0):
        return pl.BlockSpec((rows, width), lambda i, g: (0, col0 + g))

    head_row = pl.BlockSpec((1, N_HEADS), lambda i, g: (0, 0))
    state = pl.BlockSpec((1, bb, HEADS_PER_GROUP, HEAD_DIM, D_STATE), lambda i, g: (0, i, g, 0, 0))
    y, h = pl.pallas_call(
        _ssd_sample_kernel,
        grid=(DEC_BATCH // bb, N_GROUPS),
        in_specs=[cur(GROUP_CH, 0), st(GROUP_CH, 0),
                  cur(D_STATE, XBC_B_COL), st(D_STATE, XBC_B_COL),
                  cur(D_STATE, XBC_C_COL), st(D_STATE, XBC_C_COL),
                  cur(GROUP_CH, 0),
                  pl.BlockSpec((DEC_SEQ, bb, N_HEADS), lambda i, g: (0, i, 0)),
                  head_row, head_row,
                  par(CONV_W, GROUP_CH, 0), par(1, GROUP_CH, 0),
                  par(CONV_W, D_STATE, XBC_B_COL), par(1, D_STATE, XBC_B_COL),
                  par(CONV_W, D_STATE, XBC_C_COL), par(1, D_STATE, XBC_C_COL),
                  par(1, GROUP_CH, 0), par(1, GROUP_CH, 0), state],
        out_specs=[pl.BlockSpec((DEC_SEQ, bb, GROUP_CH), lambda i, g: (0, i, g)), state],
        out_shape=[jax.ShapeDtypeStruct((DEC_SEQ, DEC_BATCH, D_INNER), BF16),
                   jax.ShapeDtypeStruct((1, DEC_BATCH, N_HEADS, HEAD_DIM, D_STATE), F32)],
        compiler_params=_params(2, 56),
        name="ssd_sample",
    )(xbc_s, conv_st, xbc_s, conv_st, xbc_s, conv_st, zs_s, dt_s, dt_bias_row, a_log_row,
      conv_w, conv_b, conv_w, conv_b, conv_w, conv_b, dskip_e, ssm_norm, h0)
    return y, h


def kernel(x_prompt, x_sample, c_prompt, c_sample, state_ssm, state_conv, state_pool, w_ada, b_ada, norm_ffn1,
           w13_ffn1, w2_ffn1, norm_mix, w_in, pool_w, pool_scale, conv_w, conv_b, dt_bias, a_log, d_skip,
           ssm_norm, w_branch_pool, w_branch_ssm, w_out, norm_ffn2, w13_ffn2, w2_ffn2, norm_final):
    d = D_MODEL

    def layer0(w):
        return w.reshape(w.shape[1:])

    def row(v):
        return v.reshape(1, -1)

    xp = x_prompt.reshape(N_PROMPT, d)
    xs = x_sample.transpose(1, 0, 2).reshape(N_SAMPLE, d)

    n_c = BATCH + DEC_BATCH
    c_all = jnp.pad(jnp.concatenate([c_prompt, c_sample], axis=0), ((0, -n_c % 8), (0, 0)))
    mods = _ada(c_all, layer0(w_ada), b_ada.reshape(1, N_MOD * d))
    mods_p = mods[:BATCH]
    mods_s = mods[BATCH:n_c]

    a1 = _normmod(xp, xs, row(norm_ffn1), mods_p, mods_s, 0, 1)
    act1 = _ffn_up(a1, layer0(w13_ffn1))
    h1, u = _res_block(act1, layer0(w2_ffn1), (xp, xs), mods_p, mods_s, 2, 0.5, row(norm_mix), (3, 4),
                       tm=256, chunk=176, vmem_mib=58)

    w_in_t = layer0(w_in).T
    col_z, col_xbc, col_dt = d, d + D_INNER, d + D_INNER + CONV_DIM
    conv_buf = layer0(state_conv)
    u_pool = _in_proj(u, w_in_t, 0, d, 1024, F32)
    zs = _in_proj(u, w_in_t, col_z, D_INNER, 1024, BF16, act=_silu)
    conv_w0 = layer0(conv_w)
    conv_b0 = row(conv_b)
    xbc = _in_proj(u, w_in_t, col_xbc, CONV_DIM, 1024, F32)
    dt_raw = _in_proj(u, w_in_t, col_dt, N_HEADS, N_HEADS, F32)
    gates = _in_proj(u, w_in_t, col_dt + N_HEADS, 2 * d, 1024, BF16, act=jax.nn.sigmoid)

    pw = layer0(pool_w)
    ps = row(pool_scale)
    u_pool_s = u_pool[N_PROMPT:].reshape(DEC_SEQ, DEC_BATCH, d)
    pool_buf = layer0(state_pool)
    y_pool_p = _pool_prompt(u_pool, pw, ps)
    y_pool_s = _pool_sample(u_pool_s, pool_buf.transpose(1, 0, 2), pw, ps).reshape(N_SAMPLE, d)

    dskip_e = row(jnp.repeat(d_skip.reshape(N_HEADS), HEAD_DIM))
    norm_row = row(ssm_norm)
    y_ssm_p, prompt_ssm = _ssd_prompt(xbc, zs, dt_raw[:N_PROMPT].T, dt_bias.reshape(N_HEADS, 1),
                                      a_log.reshape(N_HEADS, 1), conv_w0, conv_b0, dskip_e, norm_row)
    xbc_s = xbc[N_PROMPT:].reshape(DEC_SEQ, DEC_BATCH, CONV_DIM)
    y_ssm_s, sample_ssm = _ssd_sample(
        xbc_s, conv_buf.transpose(1, 0, 2), zs[N_PROMPT:].reshape(DEC_SEQ, DEC_BATCH, D_INNER),
        dt_raw[N_PROMPT:].reshape(DEC_SEQ, DEC_BATCH, N_HEADS), row(dt_bias), row(a_log),
        conv_w0, conv_b0, dskip_e, norm_row, state_ssm)

    merged = _merge(y_pool_p, y_pool_s, y_ssm_p, y_ssm_s.reshape(N_SAMPLE, D_INNER), gates,
                    layer0(w_branch_pool), layer0(w_branch_ssm))
    h2, a3 = _res_block(merged, layer0(w_out), h1, mods_p, mods_s, 5, 1.0, row(norm_ffn2), (6, 7),
                        tm=512, chunk=256, vmem_mib=52)
    act2 = _ffn_up(a3, layer0(w13_ffn2))
    y_p, y_s = _res_block(act2, layer0(w2_ffn2), h2, mods_p, mods_s, 8, 0.5, row(norm_final), None,
                          tm=256, chunk=176, vmem_mib=56)

    y_prompt = y_p.reshape(BATCH, SEQ, d)
    y_sample = y_s.reshape(DEC_SEQ, DEC_BATCH, d).transpose(1, 0, 2)
    keep = CONV_W - 1
    seq_end = (jnp.arange(BATCH) + 1) * SEQ
    conv_rows = (seq_end[:, None] - keep + jnp.arange(keep)[None, :]).reshape(-1)
    pool_rows = (seq_end[:, None] - POOL_BUF + jnp.arange(POOL_BUF)[None, :]).reshape(-1)
    prompt_conv = jnp.take(xbc, conv_rows, axis=0).reshape(1, BATCH, keep, CONV_DIM)
    prompt_pool = jnp.take(u_pool, pool_rows, axis=0).reshape(1, BATCH, POOL_BUF, d)
    sample_conv = jnp.concatenate([conv_buf, xbc_s.transpose(1, 0, 2)], axis=1)[:, -keep:][None]
    sample_pool = jnp.concatenate([pool_buf, u_pool_s.transpose(1, 0, 2)], axis=1)[:, -POOL_BUF:][None]
    return (y_prompt, y_sample, prompt_ssm, prompt_conv, prompt_pool, sample_ssm, sample_conv, sample_pool)
```

```python
import functools

import jax
import jax.numpy as jnp
from jax import lax
from jax.experimental import pallas as pl
from jax.experimental.pallas import tpu as pltpu

F32 = jnp.float32
BF16 = jnp.bfloat16

D_MODEL = 2048
BATCH = 4
SEQ = 2048
DEC_BATCH = 128
DEC_SEQ = 4
PAST_LEN = 16384
POOL_WINDOWS = (2, 4, 8, 16)
POOL_GROUP = D_MODEL // len(POOL_WINDOWS)
POOL_BUF = max(POOL_WINDOWS) - 1
D_INNER = 2 * D_MODEL
HEAD_DIM = 64
N_HEADS = D_INNER // HEAD_DIM
D_STATE = 128
N_GROUPS = 8
HEADS_PER_GROUP = N_HEADS // N_GROUPS
GROUP_CH = D_INNER // N_GROUPS
CONV_W = 4
CONV_DIM = D_INNER + 2 * N_GROUPS * D_STATE
CHUNK = 128
D_FF = 256 * ((8 * D_MODEL // 3 + 255) // 256)
N_MOD = 9
EPS = 1e-6

N_PROMPT = BATCH * SEQ
N_SAMPLE = DEC_BATCH * DEC_SEQ
N_ROWS = N_PROMPT + N_SAMPLE
ROW_TILE = 512
N_ROW_TILES = N_ROWS // ROW_TILE
BIG_TILE = 1024
FULL_BIG_TILES = N_ROWS // BIG_TILE
REM_ROWS = N_ROWS - FULL_BIG_TILES * BIG_TILE
N_BIG_TILES = FULL_BIG_TILES + (1 if REM_ROWS else 0)
N_CHUNKS = SEQ // CHUNK
LANES = 128
PAIRS_PER_GROUP = HEADS_PER_GROUP // 2
SAMPLE_BB = 32
XBC_B0 = D_INNER
XBC_C0 = D_INNER + N_GROUPS * D_STATE
XBC_B_COL = XBC_B0 // D_STATE
XBC_C_COL = XBC_C0 // D_STATE
CONV_HALO = 8
MIB = 1024 * 1024


def _params(n_axes, vmem_mib):
    return pltpu.CompilerParams(dimension_semantics=("arbitrary",) * n_axes,
                                vmem_limit_bytes=vmem_mib * MIB)


def _silu(x):
    return x * jax.nn.sigmoid(x)


def _row_mods(mp_ref, ms_ref, tile, tm):
    seq = jnp.minimum(tile // (SEQ // tm), BATCH - 1)
    ms = ms_ref[...]
    ms = jnp.concatenate([ms] * (tm // DEC_BATCH), axis=0)
    return jnp.where(tile >= N_PROMPT // tm, ms, mp_ref[pl.ds(seq, 1), :])


def _rows2(xp_ref, xs_ref, tile, tm):
    return jnp.where(tile >= N_PROMPT // tm, xs_ref[...], xp_ref[...])


def _spec_p(tm, width, col=0):
    last = N_PROMPT // tm - 1
    return pl.BlockSpec((tm, width), lambda i: (jnp.minimum(i, last), col))


def _spec_s(tm, width, col=0):
    first = N_PROMPT // tm
    return pl.BlockSpec((tm, width), lambda i: (jnp.maximum(i - first, 0), col))


def _mod_specs(chunk):
    return [pl.BlockSpec((BATCH, D_MODEL), lambda i: (0, chunk)),
            pl.BlockSpec((DEC_BATCH, D_MODEL), lambda i: (0, chunk))]


def _rms(x):
    return x * lax.rsqrt(jnp.mean(x * x, axis=-1, keepdims=True) + EPS)


def _ada_kernel(c_ref, w_ref, b_ref, o_ref):
    a = _silu(c_ref[...]).astype(BF16)
    o_ref[...] = jnp.dot(a, w_ref[...].astype(BF16), preferred_element_type=F32) + b_ref[...]


def _ada(c_all, w, b):
    m = c_all.shape[0]
    n = w.shape[1]
    tn = 1024
    return pl.pallas_call(
        _ada_kernel,
        grid=(n // tn,),
        in_specs=[pl.BlockSpec((m, D_MODEL), lambda j: (0, 0)),
                  pl.BlockSpec((D_MODEL, tn), lambda j: (0, j)),
                  pl.BlockSpec((1, tn), lambda j: (0, j))],
        out_specs=pl.BlockSpec((m, tn), lambda j: (0, j)),
        out_shape=jax.ShapeDtypeStruct((m, n), F32),
        compiler_params=_params(1, 40),
        name="ada_mods",
    )(c_all, w, b)


def _normmod_kernel(xp_ref, xs_ref, g_ref, shp_ref, shs_ref, scp_ref, scs_ref, o_ref):
    i = pl.program_id(0)
    sub = 64

    def run(x_ref, shift_of, scale_of):
        for r in range(0, ROW_TILE, sub):
            y = _rms(x_ref[r:r + sub, :]) * g_ref[...]
            o_ref[r:r + sub, :] = (y * (1.0 + scale_of(r)) + shift_of(r)).astype(BF16)

    @pl.when(i < N_PROMPT // ROW_TILE)
    def _():
        seq = i // (SEQ // ROW_TILE)
        shift = shp_ref[pl.ds(seq, 1), :]
        scale = scp_ref[pl.ds(seq, 1), :]
        run(xp_ref, lambda r: shift, lambda r: scale)

    @pl.when(i >= N_PROMPT // ROW_TILE)
    def _():
        def rows(ref):
            return lambda r: ref[r % DEC_BATCH:r % DEC_BATCH + sub, :]

        run(xs_ref, rows(shs_ref), rows(scs_ref))


def _normmod(xp, xs, g, mods_p, mods_s, shift_chunk, scale_chunk):
    return pl.pallas_call(
        _normmod_kernel,
        grid=(N_ROW_TILES,),
        in_specs=[_spec_p(ROW_TILE, D_MODEL), _spec_s(ROW_TILE, D_MODEL),
                  pl.BlockSpec((1, D_MODEL), lambda i: (0, 0))]
                 + _mod_specs(shift_chunk) + _mod_specs(scale_chunk),
        out_specs=pl.BlockSpec((ROW_TILE, D_MODEL), lambda i: (i, 0)),
        out_shape=jax.ShapeDtypeStruct((N_ROWS, D_MODEL), BF16),
        compiler_params=_params(1, 48),
        name="norm_modulate",
    )(xp, xs, g, mods_p, mods_s, mods_p, mods_s)


def _per_row_tile(i, body):
    @pl.when(i < FULL_BIG_TILES)
    def _():
        body(BIG_TILE)

    @pl.when(i == FULL_BIG_TILES)
    def _():
        body(REM_ROWS)


def _ffn_up_kernel(a_ref, wa_ref, wb_ref, o_ref, wa_s, wb_s):
    i = pl.program_id(1)

    @pl.when(i == 0)
    def _():
        wa_s[...] = wa_ref[...].astype(BF16)
        wb_s[...] = wb_ref[...].astype(BF16)

    def body(rows):
        a = a_ref[0:rows, :]
        ha = jnp.dot(a, wa_s[...], preferred_element_type=F32)
        hb = jnp.dot(a, wb_s[...], preferred_element_type=F32)
        o_ref[0:rows, :] = (_silu(ha) * hb).astype(BF16)

    _per_row_tile(i, body)


def _ffn_up(a, w13):
    tn = 512
    nt = D_FF // tn
    return pl.pallas_call(
        _ffn_up_kernel,
        grid=(nt, N_BIG_TILES),
        in_specs=[pl.BlockSpec((BIG_TILE, D_MODEL), lambda j, i: (i, 0)),
                  pl.BlockSpec((D_MODEL, tn), lambda j, i: (0, j)),
                  pl.BlockSpec((D_MODEL, tn), lambda j, i: (0, nt + j))],
        out_specs=pl.BlockSpec((BIG_TILE, tn), lambda j, i: (i, j)),
        out_shape=jax.ShapeDtypeStruct((N_ROWS, D_FF), BF16),
        scratch_shapes=[pltpu.VMEM((D_MODEL, tn), BF16), pltpu.VMEM((D_MODEL, tn), BF16)],
        compiler_params=_params(2, 56),
        name="ffn_up",
    )(a, w13, w13)


def _in_proj_kernel(a_ref, wt_ref, o_ref, wt_s, *, act):
    i = pl.program_id(1)

    @pl.when(i == 0)
    def _():
        wt_s[...] = wt_ref[...].astype(BF16)

    def body(rows):
        acc = lax.dot_general(a_ref[0:rows, :], wt_s[...], (((1,), (1,)), ((), ())), preferred_element_type=F32)
        if act is not None:
            acc = act(acc)
        o_ref[0:rows, :] = acc.astype(o_ref.dtype)

    _per_row_tile(i, body)


def _in_proj(a, wt, row0, nrows, tn, out_dtype, act=None):
    k = a.shape[1]
    if row0 % tn == 0:
        w_spec = pl.BlockSpec((tn, k), lambda j, i: (row0 // tn + j, 0))
    else:
        assert row0 % 8 == 0
        w_spec = pl.BlockSpec((pl.Element(tn), pl.Element(k)),
                              lambda j, i: (pl.multiple_of(row0 + j * tn, 8), 0))
    return pl.pallas_call(
        functools.partial(_in_proj_kernel, act=act),
        grid=(nrows // tn, N_BIG_TILES),
        in_specs=[pl.BlockSpec((BIG_TILE, k), lambda j, i: (i, 0)), w_spec],
        out_specs=pl.BlockSpec((BIG_TILE, tn), lambda j, i: (i, j)),
        out_shape=jax.ShapeDtypeStruct((N_ROWS, nrows), out_dtype),
        scratch_shapes=[pltpu.VMEM((tn, k), BF16)],
        compiler_params=_params(2, 56),
        name="in_proj",
    )(a, wt)


def _load_weight(w_hbm, w_s, stage, sem, chunk):
    n = w_hbm.shape[0] // chunk

    def copy(c):
        return pltpu.make_async_copy(w_hbm.at[pl.ds(c * chunk, chunk), :], stage.at[c % 2], sem.at[c % 2])

    copy(0).start()
    for c in range(n):
        if c + 1 < n:
            copy(c + 1).start()
        copy(c).wait()
        w_s[c * chunk:(c + 1) * chunk, :] = stage[c % 2].astype(BF16)


def _res_block_kernel(*refs, tm, scale, two_source_res, final, chunk):
    refs = list(refs)
    a_ref, w_hbm = refs[:2]
    pos = 2
    if two_source_res:
        resp_ref, ress_ref = refs[pos:pos + 2]
        pos += 2
    else:
        res_ref = refs[pos]
        pos += 1
    gp_ref, gs_ref, g_ref = refs[pos:pos + 3]
    pos += 3
    if not final:
        shp_ref, shs_ref, scp_ref, scs_ref = refs[pos:pos + 4]
        pos += 4
    outs = refs[pos:pos + 2]
    w_s, stage, sem = refs[pos + 2:]
    i = pl.program_id(0)

    @pl.when(i == 0)
    def _():
        _load_weight(w_hbm, w_s, stage, sem, chunk)

    res = _rows2(resp_ref, ress_ref, i, tm) if two_source_res else res_ref[...]
    gate = _row_mods(gp_ref, gs_ref, i, tm)
    if scale != 1.0:
        gate = scale * gate
    h = res + gate * jnp.dot(a_ref[...], w_s[...], preferred_element_type=F32)
    y = _rms(h) * g_ref[...]
    if final:
        yp_ref, ys_ref = outs

        @pl.when(i < N_PROMPT // tm)
        def _():
            yp_ref[...] = y

        @pl.when(i >= N_PROMPT // tm)
        def _():
            ys_ref[...] = y
    else:
        h_ref, nxt_ref = outs
        h_ref[...] = h
        shift = _row_mods(shp_ref, shs_ref, i, tm)
        sc = _row_mods(scp_ref, scs_ref, i, tm)
        nxt_ref[...] = (y * (1.0 + sc) + shift).astype(BF16)


def _res_block(a, w, res, mods_p, mods_s, gate_chunk, scale, norm_g, next_chunks, tm, chunk, vmem_mib):
    k = a.shape[1]
    final = next_chunks is None
    two = isinstance(res, tuple)
    row = pl.BlockSpec((tm, D_MODEL), lambda i: (i, 0))
    in_specs = [pl.BlockSpec((tm, k), lambda i: (i, 0)), pl.BlockSpec(memory_space=pl.ANY)]
    args = [a, w]
    if two:
        in_specs += [_spec_p(tm, D_MODEL), _spec_s(tm, D_MODEL)]
        args += list(res)
    else:
        in_specs.append(row)
        args.append(res)
    in_specs += _mod_specs(gate_chunk) + [pl.BlockSpec((1, D_MODEL), lambda i: (0, 0))]
    args += [mods_p, mods_s, norm_g]
    if final:
        out_specs = [_spec_p(tm, D_MODEL), _spec_s(tm, D_MODEL)]
        out_shape = [jax.ShapeDtypeStruct((N_PROMPT, D_MODEL), F32), jax.ShapeDtypeStruct((N_SAMPLE, D_MODEL), F32)]
    else:
        in_specs += _mod_specs(next_chunks[0]) + _mod_specs(next_chunks[1])
        args += [mods_p, mods_s, mods_p, mods_s]
        out_specs = [row, row]
        out_shape = [jax.ShapeDtypeStruct((N_ROWS, D_MODEL), F32), jax.ShapeDtypeStruct((N_ROWS, D_MODEL), BF16)]
    return pl.pallas_call(
        functools.partial(_res_block_kernel, tm=tm, scale=scale, two_source_res=two, final=final, chunk=chunk),
        grid=(N_ROWS // tm,),
        in_specs=in_specs,
        out_specs=out_specs,
        out_shape=out_shape,
        scratch_shapes=[pltpu.VMEM((k, D_MODEL), BF16), pltpu.VMEM((2, chunk, D_MODEL), F32),
                        pltpu.SemaphoreType.DMA((2,))],
        compiler_params=_params(1, vmem_mib),
        name="res_block_final" if final else "res_block",
    )(*args)


def _merge_kernel(ypp_ref, yps_ref, ysp_ref, yss_ref, gp_ref, gs_ref, wp_hbm, ws_hbm, o_ref,
                  wp_s, ws_s, stage, sem, *, tm, chunk):
    i = pl.program_id(0)

    @pl.when(i == 0)
    def _():
        _load_weight(wp_hbm, wp_s, stage, sem, chunk)
        _load_weight(ws_hbm, ws_s, stage, sem, chunk)

    mp = jnp.dot(_rows2(ypp_ref, yps_ref, i, tm), wp_s[...], preferred_element_type=F32)
    ms = jnp.dot(_rows2(ysp_ref, yss_ref, i, tm), ws_s[...], preferred_element_type=F32)
    o_ref[...] = (gp_ref[...].astype(F32) * mp + gs_ref[...].astype(F32) * ms).astype(BF16)


def _merge(yp_p, yp_s, ys_p, ys_s, gates, w_bp, w_bs):
    tm, chunk = 256, 256
    return pl.pallas_call(
        functools.partial(_merge_kernel, tm=tm, chunk=chunk),
        grid=(N_ROWS // tm,),
        in_specs=[_spec_p(tm, D_MODEL), _spec_s(tm, D_MODEL), _spec_p(tm, D_INNER), _spec_s(tm, D_INNER),
                  pl.BlockSpec((tm, D_MODEL), lambda i: (i, 0)), pl.BlockSpec((tm, D_MODEL), lambda i: (i, 1)),
                  pl.BlockSpec(memory_space=pl.ANY), pl.BlockSpec(memory_space=pl.ANY)],
        out_specs=pl.BlockSpec((tm, D_MODEL), lambda i: (i, 0)),
        out_shape=jax.ShapeDtypeStruct((N_ROWS, D_MODEL), BF16),
        scratch_shapes=[pltpu.VMEM((D_MODEL, D_MODEL), BF16), pltpu.VMEM((D_INNER, D_MODEL), BF16),
                        pltpu.VMEM((2, chunk, D_MODEL), F32), pltpu.SemaphoreType.DMA((2,))],
        compiler_params=_params(1, 52),
        name="branch_merge",
    )(yp_p, yp_s, ys_p, ys_s, gates, gates, w_bp, w_bs)


def _pool_prompt_kernel(u_ref, uprev_ref, pw_ref, ps_ref, o_ref):
    tiles_per_seq = SEQ // ROW_TILE
    lt = pl.program_id(0) % tiles_per_seq
    halo = POOL_BUF + 1
    t = lt * ROW_TILE + lax.broadcasted_iota(jnp.int32, (ROW_TILE, 1), 0)
    for g, w in enumerate(POOL_WINDOWS):
        lo = g * POOL_GROUP
        cur = u_ref[:, lo:lo + POOL_GROUP]
        s = jnp.concatenate([jnp.where(lt == 0, 0.0, uprev_ref[:, lo:lo + POOL_GROUP]), cur], axis=0)
        k = 1
        while k < w:
            s = s + pltpu.roll(s, k, axis=0)
            k *= 2
        s = s[halo:halo + ROW_TILE]
        cnt = jnp.minimum(w, t + 1).astype(F32)
        pooled = (s / cnt - cur).astype(BF16)
        mixed = jnp.dot(pooled, pw_ref[g].astype(BF16), preferred_element_type=F32)
        o_ref[:, lo:lo + POOL_GROUP] = (mixed * ps_ref[:, lo:lo + POOL_GROUP]).astype(BF16)


def _pool_prompt(u_pool, pool_w, pool_scale):
    halo = POOL_BUF + 1
    per = ROW_TILE // halo
    return pl.pallas_call(
        _pool_prompt_kernel,
        grid=(N_PROMPT // ROW_TILE,),
        in_specs=[pl.BlockSpec((ROW_TILE, D_MODEL), lambda i: (i, 0)),
                  pl.BlockSpec((halo, D_MODEL), lambda i: (jnp.maximum(i * per - 1, 0), 0)),
                  pl.BlockSpec((len(POOL_WINDOWS), POOL_GROUP, POOL_GROUP), lambda i: (0, 0, 0)),
                  pl.BlockSpec((1, D_MODEL), lambda i: (0, 0))],
        out_specs=pl.BlockSpec((ROW_TILE, D_MODEL), lambda i: (i, 0)),
        out_shape=jax.ShapeDtypeStruct((N_PROMPT, D_MODEL), BF16),
        compiler_params=_params(1, 48),
        name="pool_prompt",
    )(u_pool, u_pool, pool_w, pool_scale)


def _pool_sample_kernel(u_ref, buf_ref, pw_ref, ps_ref, o_ref):
    n_prev = min(PAST_LEN, POOL_BUF)
    for g, w in enumerate(POOL_WINDOWS):
        lo = g * POOL_GROUP

        def row(k):
            if k < POOL_BUF:
                return buf_ref[k, :, lo:lo + POOL_GROUP]
            return u_ref[k - POOL_BUF, :, lo:lo + POOL_GROUP]

        pooled = []
        for t in range(DEC_SEQ):
            cur = row(POOL_BUF + t)
            s = cur
            for j in range(1, w):
                s = s + row(POOL_BUF + t - j)
            cnt = float(min(w, t + 1 + n_prev))
            pooled.append(s / cnt - cur)
        pooled = jnp.concatenate(pooled, axis=0).astype(BF16)
        mixed = jnp.dot(pooled, pw_ref[g].astype(BF16), preferred_element_type=F32)
        y = (mixed * ps_ref[:, lo:lo + POOL_GROUP]).astype(BF16)
        for t in range(DEC_SEQ):
            o_ref[t, :, lo:lo + POOL_GROUP] = y[t * SAMPLE_BB:(t + 1) * SAMPLE_BB]


def _pool_sample(u_s, buf_t, pool_w, pool_scale):
    return pl.pallas_call(
        _pool_sample_kernel,
        grid=(DEC_BATCH // SAMPLE_BB,),
        in_specs=[pl.BlockSpec((DEC_SEQ, SAMPLE_BB, D_MODEL), lambda i: (0, i, 0)),
                  pl.BlockSpec((POOL_BUF, SAMPLE_BB, D_MODEL), lambda i: (0, i, 0)),
                  pl.BlockSpec((len(POOL_WINDOWS), POOL_GROUP, POOL_GROUP), lambda i: (0, 0, 0)),
                  pl.BlockSpec((1, D_MODEL), lambda i: (0, 0))],
        out_specs=pl.BlockSpec((DEC_SEQ, SAMPLE_BB, D_MODEL), lambda i: (0, i, 0)),
        out_shape=jax.ShapeDtypeStruct((DEC_SEQ, DEC_BATCH, D_MODEL), BF16),
        compiler_params=_params(1, 40),
        name="pool_sample",
    )(u_s, buf_t, pool_w, pool_scale)


def _gated_group_norm(y, x, zs, dskip, norm):
    y = (y + x * dskip) * zs.astype(F32)
    return (_rms(y) * norm).astype(BF16)


def _ssd_prompt_step(c, xbc_ref, prev_ref, zs_ref, dt_ref, dtb_ref, alog_ref, cw_ref, cb_ref, dskip_ref, norm_ref,
                     y_ref, hout_ref, h_s, xa_ref):
    first = c == 0
    halo = CONV_HALO

    @pl.when(first)
    def _():
        h_s[...] = jnp.zeros_like(h_s)

    slab = 512
    for lo in range(0, CONV_DIM, slab):
        cols = slice(lo, lo + slab)
        ext = jnp.concatenate([jnp.where(first, 0.0, prev_ref[:, cols]), xbc_ref[:, cols]], axis=0)
        acc = ext * cw_ref[0:1, cols]
        for k in range(1, CONV_W):
            acc = pltpu.roll(acc, 1, axis=0) + ext * cw_ref[k:k + 1, cols]
        xa_ref[:, cols] = _silu(acc[halo:halo + CHUNK] + cb_ref[:, cols]).astype(BF16)

    dt_t = jax.nn.softplus(dt_ref[...] + dtb_ref[...])
    la_t = dt_t * (-jnp.exp(alog_ref[...]))
    lane_t = lax.broadcasted_iota(jnp.int32, la_t.shape, 1)
    s_t = la_t
    k = 1
    while k < CHUNK:
        s_t = s_t + jnp.where(lane_t >= k, pltpu.roll(s_t, k, axis=1), 0.0)
        k *= 2
    log2e = 1.0 / jnp.log(2.0)
    s2_t = s_t * log2e
    r2_t = (s_t - jnp.log(dt_t)) * log2e
    tail2_t = jnp.exp2(s2_t[:, CHUNK - 1:CHUNK] - r2_t)

    tri = (lax.broadcasted_iota(jnp.int32, (CHUNK, CHUNK), 0)
           >= lax.broadcasted_iota(jnp.int32, (CHUNK, CHUNK), 1))
    lo_half = lax.broadcasted_iota(jnp.int32, (CHUNK, LANES), 1) < HEAD_DIM

    for g in range(N_GROUPS):
        x = xa_ref[:, g * GROUP_CH:(g + 1) * GROUP_CH].astype(F32)
        bm_b = xa_ref[:, XBC_B0 + g * D_STATE:XBC_B0 + (g + 1) * D_STATE]
        cm_b = xa_ref[:, XBC_C0 + g * D_STATE:XBC_C0 + (g + 1) * D_STATE]
        cb = lax.dot_general(cm_b, bm_b, (((1,), (1,)), ((), ())), preferred_element_type=F32)
        bm_t = bm_b.astype(F32).T
        h_old = h_s[g]
        inter = jnp.dot(cm_b, h_old.astype(BF16), preferred_element_type=F32)

        ys, hs = [], []
        for i in range(PAIRS_PER_GROUP):
            w_pair, bt_pair, col_pair = [], [], []
            for j in (g * HEADS_PER_GROUP + 2 * i, g * HEADS_PER_GROUP + 2 * i + 1):
                colb = jnp.broadcast_to(s2_t[j:j + 1, :], (CHUNK, CHUNK)).T
                expo = jnp.where(tri, colb - r2_t[j:j + 1, :], -jnp.inf)
                w_pair.append((cb * jnp.exp2(expo)).astype(BF16))
                bt_pair.append((bm_t * tail2_t[j:j + 1, :]).astype(BF16))
                col_pair.append(colb)
            xp = x[:, i * LANES:(i + 1) * LANES]
            rhs = jnp.concatenate([jnp.where(lo_half, xp, 0.0), jnp.where(lo_half, 0.0, xp)], axis=0).astype(BF16)
            lhs = jnp.concatenate([jnp.concatenate(w_pair, axis=1), jnp.concatenate(bt_pair, axis=1)], axis=0)
            out = jnp.dot(lhs, rhs, preferred_element_type=F32)
            e_sel = jnp.exp2(jnp.where(lo_half, col_pair[0], col_pair[1]))
            ys.append(out[0:CHUNK] + inter[:, i * LANES:(i + 1) * LANES] * e_sel)
            hs.append(h_old[:, i * LANES:(i + 1) * LANES] * e_sel[CHUNK - 1:CHUNK, :] + out[CHUNK:2 * CHUNK])
        h_s[g] = jnp.concatenate(hs, axis=1)
        cols = slice(g * GROUP_CH, (g + 1) * GROUP_CH)
        y_ref[:, cols] = _gated_group_norm(jnp.concatenate(ys, axis=1), x, zs_ref[:, cols],
                                           dskip_ref[:, cols], norm_ref[:, cols])

    @pl.when(c == N_CHUNKS - 1)
    def _():
        for g in range(N_GROUPS):
            for i in range(PAIRS_PER_GROUP):
                pair = g * PAIRS_PER_GROUP + i
                h_pair = h_s[g, :, i * LANES:(i + 1) * LANES].T
                hout_ref[0, 0, 2 * pair:2 * pair + 2] = h_pair.reshape(2, HEAD_DIM, D_STATE)


def _split3(v):
    hi = v.astype(BF16)
    r1 = v - hi.astype(F32)
    mid = r1.astype(BF16)
    lo = (r1 - mid.astype(F32)).astype(BF16)
    return hi, mid, lo


def _ssd_sample_step(g, xh_ref, xst_ref, bm_ref, bst_ref, cm_ref, cst_ref, zs_ref, dt_ref, dtb_ref, alog_ref,
                     cwx_ref, cbx_ref, cwb_ref, cbb_ref, cwc_ref, cbc_ref, dskip_ref, norm_ref, h0_ref,
                     y_ref, hout_ref):
    bb = SAMPLE_BB
    rows = DEC_SEQ * bb

    def conv_silu(cur_ref, st_ref, w_ref, b_ref):
        full = [st_ref[k] for k in range(CONV_W - 1)] + [cur_ref[t] for t in range(DEC_SEQ)]
        outs = []
        for t in range(DEC_SEQ):
            acc = b_ref[...] + full[t] * w_ref[0:1, :]
            for k in range(1, CONV_W):
                acc = acc + full[t + k] * w_ref[k:k + 1, :]
            outs.append(_silu(acc))
        return outs

    x = conv_silu(xh_ref, xst_ref, cwx_ref, cbx_ref)
    bm = conv_silu(bm_ref, bst_ref, cwb_ref, cbb_ref)
    cm = conv_silu(cm_ref, cst_ref, cwc_ref, cbc_ref)

    a = -jnp.exp(alog_ref[...])
    dt = [jax.nn.softplus(dt_ref[t] + dtb_ref[...]) for t in range(DEC_SEQ)]
    s = [dt[0] * a]
    for t in range(1, DEC_SEQ):
        s.append(s[t - 1] + dt[t] * a)
    sel = (lax.broadcasted_iota(jnp.int32, (N_HEADS, GROUP_CH), 0)
           == g * HEADS_PER_GROUP + lax.broadcasted_iota(jnp.int32, (N_HEADS, GROUP_CH), 1) // HEAD_DIM)
    sel = jnp.where(sel, 1.0, 0.0).astype(BF16)

    def expand(v):
        return sum(jnp.dot(p, sel, preferred_element_type=F32) for p in _split3(v))

    dt_e = expand(jnp.concatenate(dt, axis=0))
    s_e = expand(jnp.concatenate(s, axis=0))
    dt_e = [dt_e[t * bb:(t + 1) * bb] for t in range(DEC_SEQ)]
    s_e = [s_e[t * bb:(t + 1) * bb] for t in range(DEC_SEQ)]

    c_all = jnp.concatenate(cm, axis=0).astype(BF16)
    b_all = jnp.concatenate(bm, axis=0)
    last = DEC_SEQ - 1
    xw_all = jnp.concatenate([x[t] * dt_e[t] * jnp.exp(s_e[last] - s_e[t]) for t in range(DEC_SEQ)], axis=0)
    xw_t = xw_all.T.astype(BF16)
    dec_t = jnp.concatenate([jnp.exp(s_e[last])] + [jnp.zeros((bb, GROUP_CH), F32)] * last, axis=0).T
    row_b = lax.broadcasted_iota(jnp.int32, (rows, 1), 0) % bb
    inter = jnp.zeros((rows, GROUP_CH), F32)
    for b in range(bb):
        mine = row_b == b
        h0 = h0_ref[0, b].reshape(GROUP_CH, D_STATE)
        yb = lax.dot_general(c_all, h0.astype(BF16), (((1,), (1,)), ((), ())), preferred_element_type=F32)
        inter = inter + jnp.where(mine, yb, 0.0)
        dh = jnp.dot(xw_t, jnp.where(mine, b_all, 0.0).astype(BF16), preferred_element_type=F32)
        h_new = h0 * jnp.broadcast_to(dec_t[:, b:b + 1], (GROUP_CH, D_STATE)) + dh
        hout_ref[0, b] = h_new.reshape(HEADS_PER_GROUP, HEAD_DIM, D_STATE)

    for t in range(DEC_SEQ):
        y = inter[t * bb:(t + 1) * bb] * jnp.exp(s_e[t])
        for u in range(t + 1):
            cb = jnp.sum(cm[t] * bm[u], axis=-1, keepdims=True)
            y = y + cb * jnp.exp(s_e[t] - s_e[u]) * dt_e[u] * x[u]
        y_ref[t] = _gated_group_norm(y, x[t], zs_ref[t], dskip_ref[...], norm_ref[...])


N_PROMPT_SSD_IN = 10
N_SAMPLE_SSD_IN = 19
SSD_STEPS = BATCH * N_CHUNKS
SAMPLE_SSD_STEPS = (DEC_BATCH // SAMPLE_BB) * N_GROUPS
SAMPLE_EVERY = SSD_STEPS // SAMPLE_SSD_STEPS


def _ssd_kernel(*refs):
    p_in = refs[:N_PROMPT_SSD_IN]
    s_in = refs[N_PROMPT_SSD_IN:N_PROMPT_SSD_IN + N_SAMPLE_SSD_IN]
    yp_ref, hp_ref, ys_ref, hs_ref, h_s, xa_ref = refs[N_PROMPT_SSD_IN + N_SAMPLE_SSD_IN:]
    s = pl.program_id(0)
    _ssd_prompt_step(s % N_CHUNKS, *p_in, yp_ref, hp_ref, h_s, xa_ref)

    @pl.when(s % SAMPLE_EVERY == SAMPLE_EVERY - 1)
    def _():
        _ssd_sample_step((s // SAMPLE_EVERY) % N_GROUPS, *s_in, ys_ref, hs_ref)


def _ssd(xbc, zs, dt_t, dt_bias_col, a_log_col, xbc_s, conv_st, zs_s, dt_s, dt_bias_row, a_log_row,
         conv_w, conv_b, dskip_e, ssm_norm, h0):
    assert SSD_STEPS % SAMPLE_SSD_STEPS == 0
    bb = SAMPLE_BB
    per = CHUNK // CONV_HALO

    def full(rows, width):
        return pl.BlockSpec((rows, width), lambda s: (0, 0))

    prompt_specs = [pl.BlockSpec((CHUNK, CONV_DIM), lambda s: (s, 0)),
                    pl.BlockSpec((CONV_HALO, CONV_DIM), lambda s: (jnp.maximum(s * per - 1, 0), 0)),
                    pl.BlockSpec((CHUNK, D_INNER), lambda s: (s, 0)),
                    pl.BlockSpec((N_HEADS, CHUNK), lambda s: (0, s)),
                    full(N_HEADS, 1), full(N_HEADS, 1),
                    full(CONV_W, CONV_DIM), full(1, CONV_DIM), full(1, D_INNER), full(1, D_INNER)]

    def blk(s):
        return (s // SAMPLE_EVERY) // N_GROUPS

    def grp(s):
        return (s // SAMPLE_EVERY) % N_GROUPS

    def cur(width, col0):
        return pl.BlockSpec((DEC_SEQ, bb, width), lambda s: (0, blk(s), col0 + grp(s)))

    def st(width, col0):
        return pl.BlockSpec((CONV_W - 1, bb, width), lambda s: (0, blk(s), col0 + grp(s)))

    def par(rows, width, col0):
        return pl.BlockSpec((rows, width), lambda s: (0, col0 + grp(s)))

    head_row = full(1, N_HEADS)
    state = pl.BlockSpec((1, bb, HEADS_PER_GROUP, HEAD_DIM, D_STATE), lambda s: (0, blk(s), grp(s), 0, 0))
    sample_specs = [cur(GROUP_CH, 0), st(GROUP_CH, 0),
                    cur(D_STATE, XBC_B_COL), st(D_STATE, XBC_B_COL),
                    cur(D_STATE, XBC_C_COL), st(D_STATE, XBC_C_COL),
                    cur(GROUP_CH, 0),
                    pl.BlockSpec((DEC_SEQ, bb, N_HEADS), lambda s: (0, blk(s), 0)),
                    head_row, head_row,
                    par(CONV_W, GROUP_CH, 0), par(1, GROUP_CH, 0),
                    par(CONV_W, D_STATE, XBC_B_COL), par(1, D_STATE, XBC_B_COL),
                    par(CONV_W, D_STATE, XBC_C_COL), par(1, D_STATE, XBC_C_COL),
                    par(1, GROUP_CH, 0), par(1, GROUP_CH, 0), state]
    assert len(prompt_specs) == N_PROMPT_SSD_IN and len(sample_specs) == N_SAMPLE_SSD_IN
    return pl.pallas_call(
        _ssd_kernel,
        grid=(SSD_STEPS,),
        in_specs=prompt_specs + sample_specs,
        out_specs=[pl.BlockSpec((CHUNK, D_INNER), lambda s: (s, 0)),
                   pl.BlockSpec((1, 1, N_HEADS, HEAD_DIM, D_STATE), lambda s: (0, s // N_CHUNKS, 0, 0, 0)),
                   pl.BlockSpec((DEC_SEQ, bb, GROUP_CH), lambda s: (0, blk(s), grp(s))),
                   state],
        out_shape=[jax.ShapeDtypeStruct((N_PROMPT, D_INNER), BF16),
                   jax.ShapeDtypeStruct((1, BATCH, N_HEADS, HEAD_DIM, D_STATE), F32),
                   jax.ShapeDtypeStruct((DEC_SEQ, DEC_BATCH, D_INNER), BF16),
                   jax.ShapeDtypeStruct((1, DEC_BATCH, N_HEADS, HEAD_DIM, D_STATE), F32)],
        scratch_shapes=[pltpu.VMEM((N_GROUPS, D_STATE, GROUP_CH), F32), pltpu.VMEM((CHUNK, CONV_DIM), BF16)],
        compiler_params=_params(1, 60),
        name="ssd_mixers",
    )(xbc, xbc, zs, dt_t, dt_bias_col, a_log_col, conv_w, conv_b, dskip_e, ssm_norm,
      xbc_s, conv_st, xbc_s, conv_st, xbc_s, conv_st, zs_s, dt_s, dt_bias_row, a_log_row,
      conv_w, conv_b, conv_w, conv_b, conv_w, conv_b, dskip_e, ssm_norm, h0)


def kernel(x_prompt, x_sample, c_prompt, c_sample, state_ssm, state_conv, state_pool, w_ada, b_ada, norm_ffn1,
           w13_ffn1, w2_ffn1, norm_mix, w_in, pool_w, pool_scale, conv_w, conv_b, dt_bias, a_log, d_skip,
           ssm_norm, w_branch_pool, w_branch_ssm, w_out, norm_ffn2, w13_ffn2, w2_ffn2, norm_final):
    d = D_MODEL

    def layer0(w):
        return w.reshape(w.shape[1:])

    def row(v):
        return v.reshape(1, -1)

    xp = x_prompt.reshape(N_PROMPT, d)
    xs = x_sample.transpose(1, 0, 2).reshape(N_SAMPLE, d)

    n_c = BATCH + DEC_BATCH
    c_all = jnp.pad(jnp.concatenate([c_prompt, c_sample], axis=0), ((0, -n_c % 8), (0, 0)))
    mods = _ada(c_all, layer0(w_ada), b_ada.reshape(1, N_MOD * d))
    mods_p = mods[:BATCH]
    mods_s = mods[BATCH:n_c]

    a1 = _normmod(xp, xs, row(norm_ffn1), mods_p, mods_s, 0, 1)
    act1 = _ffn_up(a1, layer0(w13_ffn1))
    h1, u = _res_block(act1, layer0(w2_ffn1), (xp, xs), mods_p, mods_s, 2, 0.5, row(norm_mix), (3, 4),
                       tm=256, chunk=176, vmem_mib=58)

    w_in_t = layer0(w_in).T
    col_z, col_xbc, col_dt = d, d + D_INNER, d + D_INNER + CONV_DIM
    conv_buf = layer0(state_conv)
    u_pool = _in_proj(u, w_in_t, 0, d, 1024, F32)
    zs = _in_proj(u, w_in_t, col_z, D_INNER, 1024, BF16, act=_silu)
    conv_w0 = layer0(conv_w)
    conv_b0 = row(conv_b)
    xbc = _in_proj(u, w_in_t, col_xbc, CONV_DIM, 1024, F32)
    dt_raw = _in_proj(u, w_in_t, col_dt, N_HEADS, N_HEADS, F32)
    gates = _in_proj(u, w_in_t, col_dt + N_HEADS, 2 * d, 1024, BF16, act=jax.nn.sigmoid)

    pw = layer0(pool_w)
    ps = row(pool_scale)
    u_pool_s = u_pool[N_PROMPT:].reshape(DEC_SEQ, DEC_BATCH, d)
    pool_buf = layer0(state_pool)
    y_pool_p = _pool_prompt(u_pool, pw, ps)
    y_pool_s = _pool_sample(u_pool_s, pool_buf.transpose(1, 0, 2), pw, ps).reshape(N_SAMPLE, d)

    dskip_e = row(jnp.repeat(d_skip.reshape(N_HEADS), HEAD_DIM))
    norm_row = row(ssm_norm)
    xbc_s = xbc[N_PROMPT:].reshape(DEC_SEQ, DEC_BATCH, CONV_DIM)
    y_ssm_p, prompt_ssm, y_ssm_s, sample_ssm = _ssd(
        xbc, zs, dt_raw[:N_PROMPT].T, dt_bias.reshape(N_HEADS, 1), a_log.reshape(N_HEADS, 1),
        xbc_s, conv_buf.transpose(1, 0, 2), zs[N_PROMPT:].reshape(DEC_SEQ, DEC_BATCH, D_INNER),
        dt_raw[N_PROMPT:].reshape(DEC_SEQ, DEC_BATCH, N_HEADS), row(dt_bias), row(a_log),
        conv_w0, conv_b0, dskip_e, norm_row, state_ssm)

    merged = _merge(y_pool_p, y_pool_s, y_ssm_p, y_ssm_s.reshape(N_SAMPLE, D_INNER), gates,
                    layer0(w_branch_pool), layer0(w_branch_ssm))
    h2, a3 = _res_block(merged, layer0(w_out), h1, mods_p, mods_s, 5, 1.0, row(norm_ffn2), (6, 7),
                        tm=512, chunk=256, vmem_mib=52)
    act2 = _ffn_up(a3, layer0(w13_ffn2))
    y_p, y_s = _res_block(act2, layer0(w2_ffn2), h2, mods_p, mods_s, 8, 0.5, row(norm_final), None,
                          tm=256, chunk=176, vmem_mib=56)

    y_prompt = y_p.reshape(BATCH, SEQ, d)
    y_sample = y_s.reshape(DEC_SEQ, DEC_BATCH, d).transpose(1, 0, 2)
    keep = CONV_W - 1
    seq_end = (jnp.arange(BATCH) + 1) * SEQ
    conv_rows = (seq_end[:, None] - keep + jnp.arange(keep)[None, :]).reshape(-1)
    pool_rows = (seq_end[:, None] - POOL_BUF + jnp.arange(POOL_BUF)[None, :]).reshape(-1)
    prompt_conv = jnp.take(xbc, conv_rows, axis=0).reshape(1, BATCH, keep, CONV_DIM)
    prompt_pool = jnp.take(u_pool, pool_rows, axis=0).reshape(1, BATCH, POOL_BUF, d)
    sample_conv = jnp.concatenate([conv_buf, xbc_s.transpose(1, 0, 2)], axis=1)[:, -keep:][None]
    sample_pool = jnp.concatenate([pool_buf, u_pool_s.transpose(1, 0, 2)], axis=1)[:, -POOL_BUF:][None]
    return (y_prompt, y_sample, prompt_ssm, prompt_conv, prompt_pool, sample_ssm, sample_conv, sample_pool)
```

```python
import functools

import jax
import jax.numpy as jnp
from jax import lax
from jax.experimental import pallas as pl
from jax.experimental.pallas import tpu as pltpu

F32 = jnp.float32
BF16 = jnp.bfloat16

D_MODEL = 2048
BATCH = 4
SEQ = 2048
DEC_BATCH = 128
DEC_SEQ = 4
PAST_LEN = 16384
POOL_WINDOWS = (2, 4, 8, 16)
POOL_GROUP = D_MODEL // len(POOL_WINDOWS)
POOL_BUF = max(POOL_WINDOWS) - 1
D_INNER = 2 * D_MODEL
HEAD_DIM = 64
N_HEADS = D_INNER // HEAD_DIM
D_STATE = 128
N_GROUPS = 8
HEADS_PER_GROUP = N_HEADS // N_GROUPS
GROUP_CH = D_INNER // N_GROUPS
CONV_W = 4
CONV_DIM = D_INNER + 2 * N_GROUPS * D_STATE
CHUNK = 128
D_FF = 256 * ((8 * D_MODEL // 3 + 255) // 256)
N_MOD = 9
EPS = 1e-6

N_PROMPT = BATCH * SEQ
N_SAMPLE = DEC_BATCH * DEC_SEQ
N_ROWS = N_PROMPT + N_SAMPLE
ROW_TILE = 512
N_ROW_TILES = N_ROWS // ROW_TILE
BIG_TILE = 1024
FULL_BIG_TILES = N_ROWS // BIG_TILE
REM_ROWS = N_ROWS - FULL_BIG_TILES * BIG_TILE
N_BIG_TILES = FULL_BIG_TILES + (1 if REM_ROWS else 0)
N_CHUNKS = SEQ // CHUNK
LANES = 128
PAIRS_PER_GROUP = HEADS_PER_GROUP // 2
SAMPLE_BB = 32
ROW_SLABS = N_ROWS // DEC_BATCH
SAMPLE_ROW_BLOCK = N_PROMPT // (DEC_BATCH * DEC_SEQ)
XBC_B0 = D_INNER
XBC_C0 = D_INNER + N_GROUPS * D_STATE
XBC_B_COL = XBC_B0 // D_STATE
XBC_C_COL = XBC_C0 // D_STATE
CONV_HALO = 8
MIB = 1024 * 1024


def _params(n_axes, vmem_mib):
    return pltpu.CompilerParams(dimension_semantics=("arbitrary",) * n_axes,
                                vmem_limit_bytes=vmem_mib * MIB)


def _silu(x):
    return x * jax.nn.sigmoid(x)


def _row_mods(mp_ref, ms_ref, tile, tm):
    seq = jnp.minimum(tile // (SEQ // tm), BATCH - 1)
    ms = ms_ref[...]
    ms = jnp.concatenate([ms] * (tm // DEC_BATCH), axis=0)
    return jnp.where(tile >= N_PROMPT // tm, ms, mp_ref[pl.ds(seq, 1), :])


def _rows2(xp_ref, xs_ref, tile, tm):
    return jnp.where(tile >= N_PROMPT // tm, xs_ref[...], xp_ref[...])


def _spec_p(tm, width, col=0):
    last = N_PROMPT // tm - 1
    return pl.BlockSpec((tm, width), lambda i: (jnp.minimum(i, last), col))


def _spec_s(tm, width, col=0):
    first = N_PROMPT // tm
    return pl.BlockSpec((tm, width), lambda i: (jnp.maximum(i - first, 0), col))


def _mod_specs(chunk):
    return [pl.BlockSpec((BATCH, D_MODEL), lambda i: (0, chunk)),
            pl.BlockSpec((DEC_BATCH, D_MODEL), lambda i: (0, chunk))]


def _rms(x):
    return x * lax.rsqrt(jnp.mean(x * x, axis=-1, keepdims=True) + EPS)


def _ada_kernel(c_ref, w_ref, b_ref, o_ref):
    a = _silu(c_ref[...]).astype(BF16)
    o_ref[...] = jnp.dot(a, w_ref[...].astype(BF16), preferred_element_type=F32) + b_ref[...]


def _ada(c_all, w, b):
    m = c_all.shape[0]
    n = w.shape[1]
    tn = 1024
    return pl.pallas_call(
        _ada_kernel,
        grid=(n // tn,),
        in_specs=[pl.BlockSpec((m, D_MODEL), lambda j: (0, 0)),
                  pl.BlockSpec((D_MODEL, tn), lambda j: (0, j)),
                  pl.BlockSpec((1, tn), lambda j: (0, j))],
        out_specs=pl.BlockSpec((m, tn), lambda j: (0, j)),
        out_shape=jax.ShapeDtypeStruct((m, n), F32),
        compiler_params=_params(1, 40),
        name="ada_mods",
    )(c_all, w, b)


def _normmod_kernel(xp_ref, xs_ref, g_ref, shp_ref, shs_ref, scp_ref, scs_ref, o_ref):
    i = pl.program_id(0)
    sub = 64

    def run(x_ref, shift_of, scale_of):
        for r in range(0, ROW_TILE, sub):
            y = _rms(x_ref[r:r + sub, :]) * g_ref[...]
            o_ref[r:r + sub, :] = (y * (1.0 + scale_of(r)) + shift_of(r)).astype(BF16)

    @pl.when(i < N_PROMPT // ROW_TILE)
    def _():
        seq = i // (SEQ // ROW_TILE)
        shift = shp_ref[pl.ds(seq, 1), :]
        scale = scp_ref[pl.ds(seq, 1), :]
        run(xp_ref, lambda r: shift, lambda r: scale)

    @pl.when(i >= N_PROMPT // ROW_TILE)
    def _():
        def rows(ref):
            return lambda r: ref[r % DEC_BATCH:r % DEC_BATCH + sub, :]

        run(xs_ref, rows(shs_ref), rows(scs_ref))


def _normmod(xp, xs, g, mods_p, mods_s, shift_chunk, scale_chunk):
    return pl.pallas_call(
        _normmod_kernel,
        grid=(N_ROW_TILES,),
        in_specs=[_spec_p(ROW_TILE, D_MODEL), _spec_s(ROW_TILE, D_MODEL),
                  pl.BlockSpec((1, D_MODEL), lambda i: (0, 0))]
                 + _mod_specs(shift_chunk) + _mod_specs(scale_chunk),
        out_specs=pl.BlockSpec((ROW_TILE, D_MODEL), lambda i: (i, 0)),
        out_shape=jax.ShapeDtypeStruct((N_ROWS, D_MODEL), BF16),
        compiler_params=_params(1, 48),
        name="norm_modulate",
    )(xp, xs, g, mods_p, mods_s, mods_p, mods_s)


def _per_row_tile(i, body):
    @pl.when(i < FULL_BIG_TILES)
    def _():
        body(BIG_TILE)

    @pl.when(i == FULL_BIG_TILES)
    def _():
        body(REM_ROWS)


def _ffn_up_kernel(a_ref, wa_ref, wb_ref, o_ref, wa_s, wb_s):
    i = pl.program_id(1)

    @pl.when(i == 0)
    def _():
        wa_s[...] = wa_ref[...].astype(BF16)
        wb_s[...] = wb_ref[...].astype(BF16)

    def body(rows):
        a = a_ref[0:rows, :]
        ha = jnp.dot(a, wa_s[...], preferred_element_type=F32)
        hb = jnp.dot(a, wb_s[...], preferred_element_type=F32)
        o_ref[0:rows, :] = (_silu(ha) * hb).astype(BF16)

    _per_row_tile(i, body)


def _ffn_up(a, w13):
    tn = 512
    nt = D_FF // tn
    return pl.pallas_call(
        _ffn_up_kernel,
        grid=(nt, N_BIG_TILES),
        in_specs=[pl.BlockSpec((BIG_TILE, D_MODEL), lambda j, i: (i, 0)),
                  pl.BlockSpec((D_MODEL, tn), lambda j, i: (0, j)),
                  pl.BlockSpec((D_MODEL, tn), lambda j, i: (0, nt + j))],
        out_specs=pl.BlockSpec((BIG_TILE, tn), lambda j, i: (i, j)),
        out_shape=jax.ShapeDtypeStruct((N_ROWS, D_FF), BF16),
        scratch_shapes=[pltpu.VMEM((D_MODEL, tn), BF16), pltpu.VMEM((D_MODEL, tn), BF16)],
        compiler_params=_params(2, 56),
        name="ffn_up",
    )(a, w13, w13)


def _in_proj_kernel(a_ref, wt_ref, o_ref, wt_s, *, act):
    i = pl.program_id(1)

    @pl.when(i == 0)
    def _():
        wt_s[...] = wt_ref[...].astype(BF16)

    def body(rows):
        acc = lax.dot_general(a_ref[0:rows, :], wt_s[...], (((1,), (1,)), ((), ())), preferred_element_type=F32)
        if act is not None:
            acc = act(acc)
        o_ref[0:rows, :] = acc.astype(o_ref.dtype)

    _per_row_tile(i, body)


def _in_proj(a, wt, row0, nrows, tn, out_dtype, act=None):
    k = a.shape[1]
    if row0 % tn == 0:
        w_spec = pl.BlockSpec((tn, k), lambda j, i: (row0 // tn + j, 0))
    else:
        assert row0 % 8 == 0
        w_spec = pl.BlockSpec((pl.Element(tn), pl.Element(k)),
                              lambda j, i: (pl.multiple_of(row0 + j * tn, 8), 0))
    return pl.pallas_call(
        functools.partial(_in_proj_kernel, act=act),
        grid=(nrows // tn, N_BIG_TILES),
        in_specs=[pl.BlockSpec((BIG_TILE, k), lambda j, i: (i, 0)), w_spec],
        out_specs=pl.BlockSpec((BIG_TILE, tn), lambda j, i: (i, j)),
        out_shape=jax.ShapeDtypeStruct((N_ROWS, nrows), out_dtype),
        scratch_shapes=[pltpu.VMEM((tn, k), BF16)],
        compiler_params=_params(2, 56),
        name="in_proj",
    )(a, wt)


def _load_weight(w_hbm, w_s, stage, sem, chunk):
    n = w_hbm.shape[0] // chunk

    def copy(c):
        return pltpu.make_async_copy(w_hbm.at[pl.ds(c * chunk, chunk), :], stage.at[c % 2], sem.at[c % 2])

    copy(0).start()
    for c in range(n):
        if c + 1 < n:
            copy(c + 1).start()
        copy(c).wait()
        w_s[c * chunk:(c + 1) * chunk, :] = stage[c % 2].astype(BF16)


def _res_block_kernel(*refs, tm, scale, two_source_res, final, chunk):
    refs = list(refs)
    a_ref, w_hbm = refs[:2]
    pos = 2
    if two_source_res:
        resp_ref, ress_ref = refs[pos:pos + 2]
        pos += 2
    else:
        res_ref = refs[pos]
        pos += 1
    gp_ref, gs_ref, g_ref = refs[pos:pos + 3]
    pos += 3
    if not final:
        shp_ref, shs_ref, scp_ref, scs_ref = refs[pos:pos + 4]
        pos += 4
    outs = refs[pos:pos + 2]
    w_s, stage, sem = refs[pos + 2:]
    i = pl.program_id(0)

    @pl.when(i == 0)
    def _():
        _load_weight(w_hbm, w_s, stage, sem, chunk)

    res = _rows2(resp_ref, ress_ref, i, tm) if two_source_res else res_ref[...]
    gate = _row_mods(gp_ref, gs_ref, i, tm)
    if scale != 1.0:
        gate = scale * gate
    h = res + gate * jnp.dot(a_ref[...], w_s[...], preferred_element_type=F32)
    y = _rms(h) * g_ref[...]
    if final:
        yp_ref, ys_ref = outs

        @pl.when(i < N_PROMPT // tm)
        def _():
            yp_ref[...] = y

        @pl.when(i >= N_PROMPT // tm)
        def _():
            ys_ref[...] = y
    else:
        h_ref, nxt_ref = outs
        h_ref[...] = h
        shift = _row_mods(shp_ref, shs_ref, i, tm)
        sc = _row_mods(scp_ref, scs_ref, i, tm)
        nxt_ref[...] = (y * (1.0 + sc) + shift).astype(BF16)


def _res_block(a, w, res, mods_p, mods_s, gate_chunk, scale, norm_g, next_chunks, tm, chunk, vmem_mib):
    k = a.shape[1]
    final = next_chunks is None
    two = isinstance(res, tuple)
    row = pl.BlockSpec((tm, D_MODEL), lambda i: (i, 0))
    in_specs = [pl.BlockSpec((tm, k), lambda i: (i, 0)), pl.BlockSpec(memory_space=pl.ANY)]
    args = [a, w]
    if two:
        in_specs += [_spec_p(tm, D_MODEL), _spec_s(tm, D_MODEL)]
        args += list(res)
    else:
        in_specs.append(row)
        args.append(res)
    in_specs += _mod_specs(gate_chunk) + [pl.BlockSpec((1, D_MODEL), lambda i: (0, 0))]
    args += [mods_p, mods_s, norm_g]
    if final:
        out_specs = [_spec_p(tm, D_MODEL), _spec_s(tm, D_MODEL)]
        out_shape = [jax.ShapeDtypeStruct((N_PROMPT, D_MODEL), F32), jax.ShapeDtypeStruct((N_SAMPLE, D_MODEL), F32)]
    else:
        in_specs += _mod_specs(next_chunks[0]) + _mod_specs(next_chunks[1])
        args += [mods_p, mods_s, mods_p, mods_s]
        out_specs = [row, row]
        out_shape = [jax.ShapeDtypeStruct((N_ROWS, D_MODEL), F32), jax.ShapeDtypeStruct((N_ROWS, D_MODEL), BF16)]
    return pl.pallas_call(
        functools.partial(_res_block_kernel, tm=tm, scale=scale, two_source_res=two, final=final, chunk=chunk),
        grid=(N_ROWS // tm,),
        in_specs=in_specs,
        out_specs=out_specs,
        out_shape=out_shape,
        scratch_shapes=[pltpu.VMEM((k, D_MODEL), BF16), pltpu.VMEM((2, chunk, D_MODEL), F32),
                        pltpu.SemaphoreType.DMA((2,))],
        compiler_params=_params(1, vmem_mib),
        name="res_block_final" if final else "res_block",
    )(*args)


def _merge_kernel(ypp_ref, yps_ref, ysp_ref, yss_ref, gp_ref, gs_ref, wp_hbm, ws_hbm, o_ref,
                  wp_s, ws_s, stage, sem, *, tm, chunk):
    i = pl.program_id(0)

    @pl.when(i == 0)
    def _():
        _load_weight(wp_hbm, wp_s, stage, sem, chunk)
        _load_weight(ws_hbm, ws_s, stage, sem, chunk)

    mp = jnp.dot(_rows2(ypp_ref, yps_ref, i, tm), wp_s[...], preferred_element_type=F32)
    ms = jnp.dot(_rows2(ysp_ref, yss_ref, i, tm), ws_s[...], preferred_element_type=F32)
    o_ref[...] = (gp_ref[...].astype(F32) * mp + gs_ref[...].astype(F32) * ms).astype(BF16)


def _merge(yp_p, yp_s, ys_p, ys_s, gates, w_bp, w_bs):
    tm, chunk = 256, 256
    return pl.pallas_call(
        functools.partial(_merge_kernel, tm=tm, chunk=chunk),
        grid=(N_ROWS // tm,),
        in_specs=[_spec_p(tm, D_MODEL), _spec_s(tm, D_MODEL), _spec_p(tm, D_INNER), _spec_s(tm, D_INNER),
                  pl.BlockSpec((tm, D_MODEL), lambda i: (i, 0)), pl.BlockSpec((tm, D_MODEL), lambda i: (i, 1)),
                  pl.BlockSpec(memory_space=pl.ANY), pl.BlockSpec(memory_space=pl.ANY)],
        out_specs=pl.BlockSpec((tm, D_MODEL), lambda i: (i, 0)),
        out_shape=jax.ShapeDtypeStruct((N_ROWS, D_MODEL), BF16),
        scratch_shapes=[pltpu.VMEM((D_MODEL, D_MODEL), BF16), pltpu.VMEM((D_INNER, D_MODEL), BF16),
                        pltpu.VMEM((2, chunk, D_MODEL), F32), pltpu.SemaphoreType.DMA((2,))],
        compiler_params=_params(1, 52),
        name="branch_merge",
    )(yp_p, yp_s, ys_p, ys_s, gates, gates, w_bp, w_bs)


def _pool_prompt_kernel(u_ref, uprev_ref, pw_ref, ps_ref, o_ref):
    tiles_per_seq = SEQ // ROW_TILE
    lt = pl.program_id(0) % tiles_per_seq
    halo = POOL_BUF + 1
    t = lt * ROW_TILE + lax.broadcasted_iota(jnp.int32, (ROW_TILE, 1), 0)
    for g, w in enumerate(POOL_WINDOWS):
        lo = g * POOL_GROUP
        cur = u_ref[:, lo:lo + POOL_GROUP]
        s = jnp.concatenate([jnp.where(lt == 0, 0.0, uprev_ref[:, lo:lo + POOL_GROUP]), cur], axis=0)
        k = 1
        while k < w:
            s = s + pltpu.roll(s, k, axis=0)
            k *= 2
        s = s[halo:halo + ROW_TILE]
        cnt = jnp.minimum(w, t + 1).astype(F32)
        pooled = (s / cnt - cur).astype(BF16)
        mixed = jnp.dot(pooled, pw_ref[g].astype(BF16), preferred_element_type=F32)
        o_ref[:, lo:lo + POOL_GROUP] = (mixed * ps_ref[:, lo:lo + POOL_GROUP]).astype(BF16)


def _pool_prompt(u_pool, pool_w, pool_scale):
    halo = POOL_BUF + 1
    per = ROW_TILE // halo
    return pl.pallas_call(
        _pool_prompt_kernel,
        grid=(N_PROMPT // ROW_TILE,),
        in_specs=[pl.BlockSpec((ROW_TILE, D_MODEL), lambda i: (i, 0)),
                  pl.BlockSpec((halo, D_MODEL), lambda i: (jnp.maximum(i * per - 1, 0), 0)),
                  pl.BlockSpec((len(POOL_WINDOWS), POOL_GROUP, POOL_GROUP), lambda i: (0, 0, 0)),
                  pl.BlockSpec((1, D_MODEL), lambda i: (0, 0))],
        out_specs=pl.BlockSpec((ROW_TILE, D_MODEL), lambda i: (i, 0)),
        out_shape=jax.ShapeDtypeStruct((N_PROMPT, D_MODEL), BF16),
        compiler_params=_params(1, 48),
        name="pool_prompt",
    )(u_pool, u_pool, pool_w, pool_scale)


def _pool_sample_kernel(u_ref, buf_ref, pw_ref, ps_ref, o_ref):
    n_prev = min(PAST_LEN, POOL_BUF)
    for g, w in enumerate(POOL_WINDOWS):
        lo = g * POOL_GROUP

        def row(k):
            if k < POOL_BUF:
                return buf_ref[k, :, lo:lo + POOL_GROUP]
            return u_ref[k - POOL_BUF, :, lo:lo + POOL_GROUP]

        pooled = []
        for t in range(DEC_SEQ):
            cur = row(POOL_BUF + t)
            s = cur
            for j in range(1, w):
                s = s + row(POOL_BUF + t - j)
            cnt = float(min(w, t + 1 + n_prev))
            pooled.append(s / cnt - cur)
        pooled = jnp.concatenate(pooled, axis=0).astype(BF16)
        mixed = jnp.dot(pooled, pw_ref[g].astype(BF16), preferred_element_type=F32)
        y = (mixed * ps_ref[:, lo:lo + POOL_GROUP]).astype(BF16)
        for t in range(DEC_SEQ):
            o_ref[t, :, lo:lo + POOL_GROUP] = y[t * SAMPLE_BB:(t + 1) * SAMPLE_BB]


def _pool_sample(u_s, buf_t, pool_w, pool_scale):
    return pl.pallas_call(
        _pool_sample_kernel,
        grid=(DEC_BATCH // SAMPLE_BB,),
        in_specs=[pl.BlockSpec((DEC_SEQ, SAMPLE_BB, D_MODEL), lambda i: (SAMPLE_ROW_BLOCK, i, 0)),
                  pl.BlockSpec((POOL_BUF, SAMPLE_BB, D_MODEL), lambda i: (0, i, 0)),
                  pl.BlockSpec((len(POOL_WINDOWS), POOL_GROUP, POOL_GROUP), lambda i: (0, 0, 0)),
                  pl.BlockSpec((1, D_MODEL), lambda i: (0, 0))],
        out_specs=pl.BlockSpec((DEC_SEQ, SAMPLE_BB, D_MODEL), lambda i: (0, i, 0)),
        out_shape=jax.ShapeDtypeStruct((DEC_SEQ, DEC_BATCH, D_MODEL), BF16),
        compiler_params=_params(1, 40),
        name="pool_sample",
    )(u_s, buf_t, pool_w, pool_scale)


def _gated_group_norm(y, x, zs, dskip, norm):
    y = (y + x * dskip) * zs.astype(F32)
    return (_rms(y) * norm).astype(BF16)


def _ssd_prompt_step(c, xbc_ref, prev_ref, zs_ref, dt_ref, dtb_ref, alog_ref, cw_ref, cb_ref, dskip_ref, norm_ref,
                     y_ref, hout_ref, h_s, xa_ref):
    first = c == 0
    halo = CONV_HALO

    @pl.when(first)
    def _():
        h_s[...] = jnp.zeros_like(h_s)

    assert CONV_W == 4
    slab = 512
    for lo in range(0, CONV_DIM, slab):
        cols = slice(lo, lo + slab)
        ext = jnp.concatenate([jnp.where(first, 0.0, prev_ref[:, cols]), xbc_ref[:, cols]], axis=0)
        ext1 = pltpu.roll(ext, 1, axis=0)
        near = ext * cw_ref[3:4, cols] + ext1 * cw_ref[2:3, cols]
        far = ext * cw_ref[1:2, cols] + ext1 * cw_ref[0:1, cols]
        acc = near + pltpu.roll(far, 2, axis=0)
        xa_ref[:, cols] = _silu(acc[halo:halo + CHUNK] + cb_ref[:, cols]).astype(BF16)

    dt_t = jax.nn.softplus(dt_ref[...] + dtb_ref[...])
    la_t = dt_t * (-jnp.exp(alog_ref[...]))
    lane_t = lax.broadcasted_iota(jnp.int32, la_t.shape, 1)
    s_t = la_t
    k = 1
    while k < CHUNK:
        s_t = s_t + jnp.where(lane_t >= k, pltpu.roll(s_t, k, axis=1), 0.0)
        k *= 2
    log2e = 1.0 / jnp.log(2.0)
    s2_t = s_t * log2e
    r2_t = (s_t - jnp.log(dt_t)) * log2e
    tail2_t = jnp.exp2(s2_t[:, CHUNK - 1:CHUNK] - r2_t)

    tri = (lax.broadcasted_iota(jnp.int32, (CHUNK, CHUNK), 0)
           >= lax.broadcasted_iota(jnp.int32, (CHUNK, CHUNK), 1))
    lo_half = lax.broadcasted_iota(jnp.int32, (CHUNK, LANES), 1) < HEAD_DIM

    for g in range(N_GROUPS):
        x = xa_ref[:, g * GROUP_CH:(g + 1) * GROUP_CH].astype(F32)
        bm_b = xa_ref[:, XBC_B0 + g * D_STATE:XBC_B0 + (g + 1) * D_STATE]
        cm_b = xa_ref[:, XBC_C0 + g * D_STATE:XBC_C0 + (g + 1) * D_STATE]
        cb = lax.dot_general(cm_b, bm_b, (((1,), (1,)), ((), ())), preferred_element_type=F32)
        bm_t = bm_b.astype(F32).T
        h_old = h_s[g]
        inter = jnp.dot(cm_b, h_old.astype(BF16), preferred_element_type=F32)

        ys, hs = [], []
        for i in range(PAIRS_PER_GROUP):
            w_pair, bt_pair, col_pair = [], [], []
            for j in (g * HEADS_PER_GROUP + 2 * i, g * HEADS_PER_GROUP + 2 * i + 1):
                colb = jnp.broadcast_to(s2_t[j:j + 1, :], (CHUNK, CHUNK)).T
                expo = jnp.where(tri, colb - r2_t[j:j + 1, :], -jnp.inf)
                w_pair.append((cb * jnp.exp2(expo)).astype(BF16))
                bt_pair.append((bm_t * tail2_t[j:j + 1, :]).astype(BF16))
                col_pair.append(colb)
            xp = x[:, i * LANES:(i + 1) * LANES]
            rhs = jnp.concatenate([jnp.where(lo_half, xp, 0.0), jnp.where(lo_half, 0.0, xp)], axis=0).astype(BF16)
            lhs = jnp.concatenate([jnp.concatenate(w_pair, axis=1), jnp.concatenate(bt_pair, axis=1)], axis=0)
            out = jnp.dot(lhs, rhs, preferred_element_type=F32)
            e_sel = jnp.exp2(jnp.where(lo_half, col_pair[0], col_pair[1]))
            ys.append(out[0:CHUNK] + inter[:, i * LANES:(i + 1) * LANES] * e_sel)
            hs.append(h_old[:, i * LANES:(i + 1) * LANES] * e_sel[CHUNK - 1:CHUNK, :] + out[CHUNK:2 * CHUNK])
        h_s[g] = jnp.concatenate(hs, axis=1)
        cols = slice(g * GROUP_CH, (g + 1) * GROUP_CH)
        y_ref[:, cols] = _gated_group_norm(jnp.concatenate(ys, axis=1), x, zs_ref[:, cols],
                                           dskip_ref[:, cols], norm_ref[:, cols])

    @pl.when(c == N_CHUNKS - 1)
    def _():
        for g in range(N_GROUPS):
            for i in range(PAIRS_PER_GROUP):
                pair = g * PAIRS_PER_GROUP + i
                h_pair = h_s[g, :, i * LANES:(i + 1) * LANES].T
                hout_ref[0, 0, 2 * pair:2 * pair + 2] = h_pair.reshape(2, HEAD_DIM, D_STATE)


def _split3(v):
    hi = v.astype(BF16)
    r1 = v - hi.astype(F32)
    mid = r1.astype(BF16)
    lo = (r1 - mid.astype(F32)).astype(BF16)
    return hi, mid, lo


def _ssd_sample_step(g, xh_ref, xst_ref, bm_ref, bst_ref, cm_ref, cst_ref, zs_ref, dt_ref, dtb_ref, alog_ref,
                     cwx_ref, cbx_ref, cwb_ref, cbb_ref, cwc_ref, cbc_ref, dskip_ref, norm_ref, h0_ref,
                     y_ref, hout_ref):
    bb = SAMPLE_BB
    rows = DEC_SEQ * bb

    def conv_silu(cur_ref, st_ref, w_ref, b_ref):
        full = [st_ref[k] for k in range(CONV_W - 1)] + [cur_ref[t] for t in range(DEC_SEQ)]
        outs = []
        for t in range(DEC_SEQ):
            acc = b_ref[...] + full[t] * w_ref[0:1, :]
            for k in range(1, CONV_W):
                acc = acc + full[t + k] * w_ref[k:k + 1, :]
            outs.append(_silu(acc))
        return outs

    x = conv_silu(xh_ref, xst_ref, cwx_ref, cbx_ref)
    bm = conv_silu(bm_ref, bst_ref, cwb_ref, cbb_ref)
    cm = conv_silu(cm_ref, cst_ref, cwc_ref, cbc_ref)

    a = -jnp.exp(alog_ref[...])
    dt = [jax.nn.softplus(dt_ref[t] + dtb_ref[...]) for t in range(DEC_SEQ)]
    s = [dt[0] * a]
    for t in range(1, DEC_SEQ):
        s.append(s[t - 1] + dt[t] * a)
    sel = (lax.broadcasted_iota(jnp.int32, (N_HEADS, GROUP_CH), 0)
           == g * HEADS_PER_GROUP + lax.broadcasted_iota(jnp.int32, (N_HEADS, GROUP_CH), 1) // HEAD_DIM)
    sel = jnp.where(sel, 1.0, 0.0).astype(BF16)

    def expand(v):
        return sum(jnp.dot(p, sel, preferred_element_type=F32) for p in _split3(v))

    dt_e = expand(jnp.concatenate(dt, axis=0))
    s_e = expand(jnp.concatenate(s, axis=0))
    dt_e = [dt_e[t * bb:(t + 1) * bb] for t in range(DEC_SEQ)]
    s_e = [s_e[t * bb:(t + 1) * bb] for t in range(DEC_SEQ)]

    c_all = jnp.concatenate(cm, axis=0).astype(BF16)
    b_all = jnp.concatenate(bm, axis=0)
    last = DEC_SEQ - 1
    xw_all = jnp.concatenate([x[t] * dt_e[t] * jnp.exp(s_e[last] - s_e[t]) for t in range(DEC_SEQ)], axis=0)
    xw_t = xw_all.T.astype(BF16)
    dec_t = jnp.concatenate([jnp.exp(s_e[last])] + [jnp.zeros((bb, GROUP_CH), F32)] * last, axis=0).T
    row_b = lax.broadcasted_iota(jnp.int32, (rows, 1), 0) % bb
    inter = jnp.zeros((rows, GROUP_CH), F32)
    for b in range(bb):
        mine = row_b == b
        h0 = h0_ref[0, b].reshape(GROUP_CH, D_STATE)
        yb = lax.dot_general(c_all, h0.astype(BF16), (((1,), (1,)), ((), ())), preferred_element_type=F32)
        inter = inter + jnp.where(mine, yb, 0.0)
        dh = jnp.dot(xw_t, jnp.where(mine, b_all, 0.0).astype(BF16), preferred_element_type=F32)
        h_new = h0 * jnp.broadcast_to(dec_t[:, b:b + 1], (GROUP_CH, D_STATE)) + dh
        hout_ref[0, b] = h_new.reshape(HEADS_PER_GROUP, HEAD_DIM, D_STATE)

    for t in range(DEC_SEQ):
        y = inter[t * bb:(t + 1) * bb] * jnp.exp(s_e[t])
        for u in range(t + 1):
            cb = jnp.sum(cm[t] * bm[u], axis=-1, keepdims=True)
            y = y + cb * jnp.exp(s_e[t] - s_e[u]) * dt_e[u] * x[u]
        y_ref[t] = _gated_group_norm(y, x[t], zs_ref[t], dskip_ref[...], norm_ref[...])


N_PROMPT_SSD_IN = 10
N_SAMPLE_SSD_IN = 19
SSD_STEPS = BATCH * N_CHUNKS
SAMPLE_SSD_STEPS = (DEC_BATCH // SAMPLE_BB) * N_GROUPS
SAMPLE_EVERY = SSD_STEPS // SAMPLE_SSD_STEPS


def _ssd_kernel(*refs):
    p_in = refs[:N_PROMPT_SSD_IN]
    s_in = refs[N_PROMPT_SSD_IN:N_PROMPT_SSD_IN + N_SAMPLE_SSD_IN]
    yp_ref, hp_ref, ys_ref, hs_ref, h_s, xa_ref = refs[N_PROMPT_SSD_IN + N_SAMPLE_SSD_IN:]
    s = pl.program_id(0)
    _ssd_prompt_step(s % N_CHUNKS, *p_in, yp_ref, hp_ref, h_s, xa_ref)

    @pl.when(s % SAMPLE_EVERY == SAMPLE_EVERY - 1)
    def _():
        _ssd_sample_step((s // SAMPLE_EVERY) % N_GROUPS, *s_in, ys_ref, hs_ref)


def _ssd(xbc, zs, dt_t, dt_bias_col, a_log_col, xbc_s, conv_st, zs_s, dt_s, dt_bias_row, a_log_row,
         conv_w, conv_b, dskip_e, ssm_norm, h0):
    assert SSD_STEPS % SAMPLE_SSD_STEPS == 0
    bb = SAMPLE_BB
    per = CHUNK // CONV_HALO

    def full(rows, width):
        return pl.BlockSpec((rows, width), lambda s: (0, 0))

    prompt_specs = [pl.BlockSpec((CHUNK, CONV_DIM), lambda s: (s, 0)),
                    pl.BlockSpec((CONV_HALO, CONV_DIM), lambda s: (jnp.maximum(s * per - 1, 0), 0)),
                    pl.BlockSpec((CHUNK, D_INNER), lambda s: (s, 0)),
                    pl.BlockSpec((N_HEADS, CHUNK), lambda s: (0, s)),
                    full(N_HEADS, 1), full(N_HEADS, 1),
                    full(CONV_W, CONV_DIM), full(1, CONV_DIM), full(1, D_INNER), full(1, D_INNER)]

    def blk(s):
        return (s // SAMPLE_EVERY) // N_GROUPS

    def grp(s):
        return (s // SAMPLE_EVERY) % N_GROUPS

    def cur(width, col0):
        return pl.BlockSpec((DEC_SEQ, bb, width), lambda s: (SAMPLE_ROW_BLOCK, blk(s), col0 + grp(s)))

    def st(width, col0):
        return pl.BlockSpec((CONV_W - 1, bb, width), lambda s: (0, blk(s), col0 + grp(s)))

    def par(rows, width, col0):
        return pl.BlockSpec((rows, width), lambda s: (0, col0 + grp(s)))

    head_row = full(1, N_HEADS)
    state = pl.BlockSpec((1, bb, HEADS_PER_GROUP, HEAD_DIM, D_STATE), lambda s: (0, blk(s), grp(s), 0, 0))
    sample_specs = [cur(GROUP_CH, 0), st(GROUP_CH, 0),
                    cur(D_STATE, XBC_B_COL), st(D_STATE, XBC_B_COL),
                    cur(D_STATE, XBC_C_COL), st(D_STATE, XBC_C_COL),
                    cur(GROUP_CH, 0),
                    pl.BlockSpec((DEC_SEQ, bb, N_HEADS), lambda s: (SAMPLE_ROW_BLOCK, blk(s), 0)),
                    head_row, head_row,
                    par(CONV_W, GROUP_CH, 0), par(1, GROUP_CH, 0),
                    par(CONV_W, D_STATE, XBC_B_COL), par(1, D_STATE, XBC_B_COL),
                    par(CONV_W, D_STATE, XBC_C_COL), par(1, D_STATE, XBC_C_COL),
                    par(1, GROUP_CH, 0), par(1, GROUP_CH, 0), state]
    assert len(prompt_specs) == N_PROMPT_SSD_IN and len(sample_specs) == N_SAMPLE_SSD_IN
    return pl.pallas_call(
        _ssd_kernel,
        grid=(SSD_STEPS,),
        in_specs=prompt_specs + sample_specs,
        out_specs=[pl.BlockSpec((CHUNK, D_INNER), lambda s: (s, 0)),
                   pl.BlockSpec((1, 1, N_HEADS, HEAD_DIM, D_STATE), lambda s: (0, s // N_CHUNKS, 0, 0, 0)),
                   pl.BlockSpec((DEC_SEQ, bb, GROUP_CH), lambda s: (0, blk(s), grp(s))),
                   state],
        out_shape=[jax.ShapeDtypeStruct((N_PROMPT, D_INNER), BF16),
                   jax.ShapeDtypeStruct((1, BATCH, N_HEADS, HEAD_DIM, D_STATE), F32),
                   jax.ShapeDtypeStruct((DEC_SEQ, DEC_BATCH, D_INNER), BF16),
                   jax.ShapeDtypeStruct((1, DEC_BATCH, N_HEADS, HEAD_DIM, D_STATE), F32)],
        scratch_shapes=[pltpu.VMEM((N_GROUPS, D_STATE, GROUP_CH), F32), pltpu.VMEM((CHUNK, CONV_DIM), BF16)],
        compiler_params=_params(1, 60),
        name="ssd_mixers",
    )(xbc, xbc, zs, dt_t, dt_bias_col, a_log_col, conv_w, conv_b, dskip_e, ssm_norm,
      xbc_s, conv_st, xbc_s, conv_st, xbc_s, conv_st, zs_s, dt_s, dt_bias_row, a_log_row,
      conv_w, conv_b, conv_w, conv_b, conv_w, conv_b, dskip_e, ssm_norm, h0)


def kernel(x_prompt, x_sample, c_prompt, c_sample, state_ssm, state_conv, state_pool, w_ada, b_ada, norm_ffn1,
           w13_ffn1, w2_ffn1, norm_mix, w_in, pool_w, pool_scale, conv_w, conv_b, dt_bias, a_log, d_skip,
           ssm_norm, w_branch_pool, w_branch_ssm, w_out, norm_ffn2, w13_ffn2, w2_ffn2, norm_final):
    d = D_MODEL

    def layer0(w):
        return w.reshape(w.shape[1:])

    def row(v):
        return v.reshape(1, -1)

    xp = x_prompt.reshape(N_PROMPT, d)
    xs = x_sample.transpose(1, 0, 2).reshape(N_SAMPLE, d)

    n_c = BATCH + DEC_BATCH
    c_all = jnp.pad(jnp.concatenate([c_prompt, c_sample], axis=0), ((0, -n_c % 8), (0, 0)))
    mods = _ada(c_all, layer0(w_ada), b_ada.reshape(1, N_MOD * d))
    mods_p = mods[:BATCH]
    mods_s = mods[BATCH:n_c]

    a1 = _normmod(xp, xs, row(norm_ffn1), mods_p, mods_s, 0, 1)
    act1 = _ffn_up(a1, layer0(w13_ffn1))
    h1, u = _res_block(act1, layer0(w2_ffn1), (xp, xs), mods_p, mods_s, 2, 0.5, row(norm_mix), (3, 4),
                       tm=256, chunk=176, vmem_mib=58)

    w_in_t = layer0(w_in).T
    col_z, col_xbc, col_dt = d, d + D_INNER, d + D_INNER + CONV_DIM
    conv_buf = layer0(state_conv)
    u_pool = _in_proj(u, w_in_t, 0, d, 1024, F32)
    zs = _in_proj(u, w_in_t, col_z, D_INNER, 1024, BF16, act=_silu)
    conv_w0 = layer0(conv_w)
    conv_b0 = row(conv_b)
    xbc = _in_proj(u, w_in_t, col_xbc, CONV_DIM, 1024, F32)
    dt_raw = _in_proj(u, w_in_t, col_dt, N_HEADS, N_HEADS, F32)
    gates = _in_proj(u, w_in_t, col_dt + N_HEADS, 2 * d, 1024, BF16, act=jax.nn.sigmoid)

    pw = layer0(pool_w)
    ps = row(pool_scale)
    u_pool_s = u_pool[N_PROMPT:].reshape(DEC_SEQ, DEC_BATCH, d)
    pool_buf = layer0(state_pool)
    y_pool_p = _pool_prompt(u_pool, pw, ps)
    y_pool_s = _pool_sample(u_pool.reshape(ROW_SLABS, DEC_BATCH, d), pool_buf.transpose(1, 0, 2),
                            pw, ps).reshape(N_SAMPLE, d)

    dskip_e = row(jnp.repeat(d_skip.reshape(N_HEADS), HEAD_DIM))
    norm_row = row(ssm_norm)
    y_ssm_p, prompt_ssm, y_ssm_s, sample_ssm = _ssd(
        xbc, zs, dt_raw[:N_PROMPT].T, dt_bias.reshape(N_HEADS, 1), a_log.reshape(N_HEADS, 1),
        xbc.reshape(ROW_SLABS, DEC_BATCH, CONV_DIM), conv_buf.transpose(1, 0, 2),
        zs.reshape(ROW_SLABS, DEC_BATCH, D_INNER), dt_raw.reshape(ROW_SLABS, DEC_BATCH, N_HEADS),
        row(dt_bias), row(a_log), conv_w0, conv_b0, dskip_e, norm_row, state_ssm)

    merged = _merge(y_pool_p, y_pool_s, y_ssm_p, y_ssm_s.reshape(N_SAMPLE, D_INNER), gates,
                    layer0(w_branch_pool), layer0(w_branch_ssm))
    h2, a3 = _res_block(merged, layer0(w_out), h1, mods_p, mods_s, 5, 1.0, row(norm_ffn2), (6, 7),
                        tm=512, chunk=256, vmem_mib=52)
    act2 = _ffn_up(a3, layer0(w13_ffn2))
    y_p, y_s = _res_block(act2, layer0(w2_ffn2), h2, mods_p, mods_s, 8, 0.5, row(norm_final), None,
                          tm=256, chunk=176, vmem_mib=56)

    y_prompt = y_p.reshape(BATCH, SEQ, d)
    y_sample = y_s.reshape(DEC_SEQ, DEC_BATCH, d).transpose(1, 0, 2)
    keep = CONV_W - 1
    seq_end = (jnp.arange(BATCH) + 1) * SEQ
    conv_rows = (seq_end[:, None] - keep + jnp.arange(keep)[None, :]).reshape(-1)
    pool_rows = (seq_end[:, None] - POOL_BUF + jnp.arange(POOL_BUF)[None, :]).reshape(-1)
    prompt_conv = jnp.take(xbc, conv_rows, axis=0).reshape(1, BATCH, keep, CONV_DIM)
    prompt_pool = jnp.take(u_pool, pool_rows, axis=0).reshape(1, BATCH, POOL_BUF, d)
    assert DEC_SEQ >= keep
    xbc_last = xbc[N_ROWS - keep * DEC_BATCH:].reshape(keep, DEC_BATCH, CONV_DIM)
    sample_conv = xbc_last.transpose(1, 0, 2)[None]
    sample_pool = jnp.concatenate([pool_buf, u_pool_s.transpose(1, 0, 2)], axis=1)[:, -POOL_BUF:][None]
    return (y_prompt, y_sample, prompt_ssm, prompt_conv, prompt_pool, sample_ssm, sample_conv, sample_pool)
```

```python
import functools

import jax
import jax.numpy as jnp
from jax import lax
from jax.experimental import pallas as pl
from jax.experimental.pallas import tpu as pltpu

F32 = jnp.float32
BF16 = jnp.bfloat16

D_MODEL = 2048
BATCH = 4
SEQ = 2048
DEC_BATCH = 128
DEC_SEQ = 4
PAST_LEN = 16384
POOL_WINDOWS = (2, 4, 8, 16)
POOL_GROUP = D_MODEL // len(POOL_WINDOWS)
POOL_BUF = max(POOL_WINDOWS) - 1
D_INNER = 2 * D_MODEL
HEAD_DIM = 64
N_HEADS = D_INNER // HEAD_DIM
D_STATE = 128
N_GROUPS = 8
HEADS_PER_GROUP = N_HEADS // N_GROUPS
GROUP_CH = D_INNER // N_GROUPS
CONV_W = 4
CONV_DIM = D_INNER + 2 * N_GROUPS * D_STATE
CHUNK = 128
D_FF = 256 * ((8 * D_MODEL // 3 + 255) // 256)
N_MOD = 9
EPS = 1e-6

N_PROMPT = BATCH * SEQ
N_SAMPLE = DEC_BATCH * DEC_SEQ
N_ROWS = N_PROMPT + N_SAMPLE
ROW_TILE = 512
N_ROW_TILES = N_ROWS // ROW_TILE
BIG_TILE = 1024
FULL_BIG_TILES = N_ROWS // BIG_TILE
REM_ROWS = N_ROWS - FULL_BIG_TILES * BIG_TILE
N_BIG_TILES = FULL_BIG_TILES + (1 if REM_ROWS else 0)
N_CHUNKS = SEQ // CHUNK
LANES = 128
PAIRS_PER_GROUP = HEADS_PER_GROUP // 2
SAMPLE_BB = 32
ROW_SLABS = N_ROWS // DEC_BATCH
SAMPLE_ROW_BLOCK = N_PROMPT // (DEC_BATCH * DEC_SEQ)
XBC_B0 = D_INNER
XBC_C0 = D_INNER + N_GROUPS * D_STATE
XBC_B_COL = XBC_B0 // D_STATE
XBC_C_COL = XBC_C0 // D_STATE
CONV_HALO = 8
MIB = 1024 * 1024


def _params(n_axes, vmem_mib):
    return pltpu.CompilerParams(dimension_semantics=("arbitrary",) * n_axes,
                                vmem_limit_bytes=vmem_mib * MIB)


def _sigmoid(x):
    return 0.5 * jnp.tanh(0.5 * x) + 0.5


def _silu(x):
    return x * _sigmoid(x)


def _row_mods(mp_ref, ms_ref, tile, tm):
    seq = jnp.minimum(tile // (SEQ // tm), BATCH - 1)
    ms = ms_ref[...]
    ms = jnp.concatenate([ms] * (tm // DEC_BATCH), axis=0)
    return jnp.where(tile >= N_PROMPT // tm, ms, mp_ref[pl.ds(seq, 1), :])


def _rows2(xp_ref, xs_ref, tile, tm):
    return jnp.where(tile >= N_PROMPT // tm, xs_ref[...], xp_ref[...])


def _spec_p(tm, width, col=0):
    last = N_PROMPT // tm - 1
    return pl.BlockSpec((tm, width), lambda i: (jnp.minimum(i, last), col))


def _spec_s(tm, width, col=0):
    first = N_PROMPT // tm
    return pl.BlockSpec((tm, width), lambda i: (jnp.maximum(i - first, 0), col))


def _mod_specs(chunk):
    return [pl.BlockSpec((BATCH, D_MODEL), lambda i: (0, chunk)),
            pl.BlockSpec((DEC_BATCH, D_MODEL), lambda i: (0, chunk))]


def _rms(x):
    return x * lax.rsqrt(jnp.mean(x * x, axis=-1, keepdims=True) + EPS)


def _ada_kernel(c_ref, w_ref, b_ref, o_ref):
    a = _silu(c_ref[...]).astype(BF16)
    o_ref[...] = jnp.dot(a, w_ref[...].astype(BF16), preferred_element_type=F32) + b_ref[...]


def _ada(c_all, w, b):
    m = c_all.shape[0]
    n = w.shape[1]
    tn = 1024
    return pl.pallas_call(
        _ada_kernel,
        grid=(n // tn,),
        in_specs=[pl.BlockSpec((m, D_MODEL), lambda j: (0, 0)),
                  pl.BlockSpec((D_MODEL, tn), lambda j: (0, j)),
                  pl.BlockSpec((1, tn), lambda j: (0, j))],
        out_specs=pl.BlockSpec((m, tn), lambda j: (0, j)),
        out_shape=jax.ShapeDtypeStruct((m, n), F32),
        compiler_params=_params(1, 40),
        name="ada_mods",
    )(c_all, w, b)


def _normmod_kernel(xp_ref, xs_ref, g_ref, shp_ref, shs_ref, scp_ref, scs_ref, o_ref):
    i = pl.program_id(0)
    sub = 64

    def run(x_ref, shift_of, scale_of):
        for r in range(0, ROW_TILE, sub):
            y = _rms(x_ref[r:r + sub, :]) * g_ref[...]
            o_ref[r:r + sub, :] = (y * (1.0 + scale_of(r)) + shift_of(r)).astype(BF16)

    @pl.when(i < N_PROMPT // ROW_TILE)
    def _():
        seq = i // (SEQ // ROW_TILE)
        shift = shp_ref[pl.ds(seq, 1), :]
        scale = scp_ref[pl.ds(seq, 1), :]
        run(xp_ref, lambda r: shift, lambda r: scale)

    @pl.when(i >= N_PROMPT // ROW_TILE)
    def _():
        def rows(ref):
            return lambda r: ref[r % DEC_BATCH:r % DEC_BATCH + sub, :]

        run(xs_ref, rows(shs_ref), rows(scs_ref))


def _normmod(xp, xs, g, mods_p, mods_s, shift_chunk, scale_chunk):
    return pl.pallas_call(
        _normmod_kernel,
        grid=(N_ROW_TILES,),
        in_specs=[_spec_p(ROW_TILE, D_MODEL), _spec_s(ROW_TILE, D_MODEL),
                  pl.BlockSpec((1, D_MODEL), lambda i: (0, 0))]
                 + _mod_specs(shift_chunk) + _mod_specs(scale_chunk),
        out_specs=pl.BlockSpec((ROW_TILE, D_MODEL), lambda i: (i, 0)),
        out_shape=jax.ShapeDtypeStruct((N_ROWS, D_MODEL), BF16),
        compiler_params=_params(1, 48),
        name="norm_modulate",
    )(xp, xs, g, mods_p, mods_s, mods_p, mods_s)


def _per_row_tile(i, body):
    @pl.when(i < FULL_BIG_TILES)
    def _():
        body(BIG_TILE)

    @pl.when(i == FULL_BIG_TILES)
    def _():
        body(REM_ROWS)


def _ffn_up_kernel(a_ref, wa_ref, wb_ref, o_ref, wa_s, wb_s):
    i = pl.program_id(1)

    @pl.when(i == 0)
    def _():
        wa_s[...] = wa_ref[...].astype(BF16)
        wb_s[...] = wb_ref[...].astype(BF16)

    def body(rows):
        a = a_ref[0:rows, :]
        ha = jnp.dot(a, wa_s[...], preferred_element_type=F32)
        hb = jnp.dot(a, wb_s[...], preferred_element_type=F32)
        o_ref[0:rows, :] = (_silu(ha) * hb).astype(BF16)

    _per_row_tile(i, body)


def _ffn_up(a, w13):
    tn = 512
    nt = D_FF // tn
    return pl.pallas_call(
        _ffn_up_kernel,
        grid=(nt, N_BIG_TILES),
        in_specs=[pl.BlockSpec((BIG_TILE, D_MODEL), lambda j, i: (i, 0)),
                  pl.BlockSpec((D_MODEL, tn), lambda j, i: (0, j)),
                  pl.BlockSpec((D_MODEL, tn), lambda j, i: (0, nt + j))],
        out_specs=pl.BlockSpec((BIG_TILE, tn), lambda j, i: (i, j)),
        out_shape=jax.ShapeDtypeStruct((N_ROWS, D_FF), BF16),
        scratch_shapes=[pltpu.VMEM((D_MODEL, tn), BF16), pltpu.VMEM((D_MODEL, tn), BF16)],
        compiler_params=_params(2, 56),
        name="ffn_up",
    )(a, w13, w13)


def _in_proj_kernel(a_ref, wt_ref, o_ref, wt_s, *, act):
    i = pl.program_id(1)

    @pl.when(i == 0)
    def _():
        wt_s[...] = wt_ref[...].astype(BF16)

    def body(rows):
        acc = lax.dot_general(a_ref[0:rows, :], wt_s[...], (((1,), (1,)), ((), ())), preferred_element_type=F32)
        if act is not None:
            acc = act(acc)
        o_ref[0:rows, :] = acc.astype(o_ref.dtype)

    _per_row_tile(i, body)


def _in_proj(a, wt, row0, nrows, tn, out_dtype, act=None):
    k = a.shape[1]
    if row0 % tn == 0:
        w_spec = pl.BlockSpec((tn, k), lambda j, i: (row0 // tn + j, 0))
    else:
        assert row0 % 8 == 0
        w_spec = pl.BlockSpec((pl.Element(tn), pl.Element(k)),
                              lambda j, i: (pl.multiple_of(row0 + j * tn, 8), 0))
    return pl.pallas_call(
        functools.partial(_in_proj_kernel, act=act),
        grid=(nrows // tn, N_BIG_TILES),
        in_specs=[pl.BlockSpec((BIG_TILE, k), lambda j, i: (i, 0)), w_spec],
        out_specs=pl.BlockSpec((BIG_TILE, tn), lambda j, i: (i, j)),
        out_shape=jax.ShapeDtypeStruct((N_ROWS, nrows), out_dtype),
        scratch_shapes=[pltpu.VMEM((tn, k), BF16)],
        compiler_params=_params(2, 56),
        name="in_proj",
    )(a, wt)


def _load_weight(w_hbm, w_s, stage, sem, chunk):
    n = w_hbm.shape[0] // chunk

    def copy(c):
        return pltpu.make_async_copy(w_hbm.at[pl.ds(c * chunk, chunk), :], stage.at[c % 2], sem.at[c % 2])

    copy(0).start()
    for c in range(n):
        if c + 1 < n:
            copy(c + 1).start()
        copy(c).wait()
        w_s[c * chunk:(c + 1) * chunk, :] = stage[c % 2].astype(BF16)


def _res_block_kernel(*refs, tm, scale, two_source_res, final, chunk):
    refs = list(refs)
    a_ref, w_hbm = refs[:2]
    pos = 2
    if two_source_res:
        resp_ref, ress_ref = refs[pos:pos + 2]
        pos += 2
    else:
        res_ref = refs[pos]
        pos += 1
    gp_ref, gs_ref, g_ref = refs[pos:pos + 3]
    pos += 3
    if not final:
        shp_ref, shs_ref, scp_ref, scs_ref = refs[pos:pos + 4]
        pos += 4
    outs = refs[pos:pos + 2]
    w_s, stage, sem = refs[pos + 2:]
    i = pl.program_id(0)

    @pl.when(i == 0)
    def _():
        _load_weight(w_hbm, w_s, stage, sem, chunk)

    res = _rows2(resp_ref, ress_ref, i, tm) if two_source_res else res_ref[...]
    gate = _row_mods(gp_ref, gs_ref, i, tm)
    if scale != 1.0:
        gate = scale * gate
    h = res + gate * jnp.dot(a_ref[...], w_s[...], preferred_element_type=F32)
    y = _rms(h) * g_ref[...]
    if final:
        yp_ref, ys_ref = outs

        @pl.when(i < N_PROMPT // tm)
        def _():
            yp_ref[...] = y

        @pl.when(i >= N_PROMPT // tm)
        def _():
            ys_ref[...] = y
    else:
        h_ref, nxt_ref = outs
        h_ref[...] = h
        shift = _row_mods(shp_ref, shs_ref, i, tm)
        sc = _row_mods(scp_ref, scs_ref, i, tm)
        nxt_ref[...] = (y * (1.0 + sc) + shift).astype(BF16)


def _res_block(a, w, res, mods_p, mods_s, gate_chunk, scale, norm_g, next_chunks, tm, chunk, vmem_mib):
    k = a.shape[1]
    final = next_chunks is None
    two = isinstance(res, tuple)
    row = pl.BlockSpec((tm, D_MODEL), lambda i: (i, 0))
    in_specs = [pl.BlockSpec((tm, k), lambda i: (i, 0)), pl.BlockSpec(memory_space=pl.ANY)]
    args = [a, w]
    if two:
        in_specs += [_spec_p(tm, D_MODEL), _spec_s(tm, D_MODEL)]
        args += list(res)
    else:
        in_specs.append(row)
        args.append(res)
    in_specs += _mod_specs(gate_chunk) + [pl.BlockSpec((1, D_MODEL), lambda i: (0, 0))]
    args += [mods_p, mods_s, norm_g]
    if final:
        out_specs = [_spec_p(tm, D_MODEL), _spec_s(tm, D_MODEL)]
        out_shape = [jax.ShapeDtypeStruct((N_PROMPT, D_MODEL), F32), jax.ShapeDtypeStruct((N_SAMPLE, D_MODEL), F32)]
    else:
        in_specs += _mod_specs(next_chunks[0]) + _mod_specs(next_chunks[1])
        args += [mods_p, mods_s, mods_p, mods_s]
        out_specs = [row, row]
        out_shape = [jax.ShapeDtypeStruct((N_ROWS, D_MODEL), F32), jax.ShapeDtypeStruct((N_ROWS, D_MODEL), BF16)]
    return pl.pallas_call(
        functools.partial(_res_block_kernel, tm=tm, scale=scale, two_source_res=two, final=final, chunk=chunk),
        grid=(N_ROWS // tm,),
        in_specs=in_specs,
        out_specs=out_specs,
        out_shape=out_shape,
        scratch_shapes=[pltpu.VMEM((k, D_MODEL), BF16), pltpu.VMEM((2, chunk, D_MODEL), F32),
                        pltpu.SemaphoreType.DMA((2,))],
        compiler_params=_params(1, vmem_mib),
        name="res_block_final" if final else "res_block",
    )(*args)


def _merge_kernel(ypp_ref, yps_ref, ysp_ref, yss_ref, gp_ref, gs_ref, wp_hbm, ws_hbm, o_ref,
                  wp_s, ws_s, stage, sem, *, tm, chunk):
    i = pl.program_id(0)

    @pl.when(i == 0)
    def _():
        _load_weight(wp_hbm, wp_s, stage, sem, chunk)
        _load_weight(ws_hbm, ws_s, stage, sem, chunk)

    mp = jnp.dot(_rows2(ypp_ref, yps_ref, i, tm), wp_s[...], preferred_element_type=F32)
    ms = jnp.dot(_rows2(ysp_ref, yss_ref, i, tm), ws_s[...], preferred_element_type=F32)
    o_ref[...] = (gp_ref[...].astype(F32) * mp + gs_ref[...].astype(F32) * ms).astype(BF16)


def _merge(yp_p, yp_s, ys_p, ys_s, gates, w_bp, w_bs):
    tm, chunk = 256, 256
    return pl.pallas_call(
        functools.partial(_merge_kernel, tm=tm, chunk=chunk),
        grid=(N_ROWS // tm,),
        in_specs=[_spec_p(tm, D_MODEL), _spec_s(tm, D_MODEL), _spec_p(tm, D_INNER), _spec_s(tm, D_INNER),
                  pl.BlockSpec((tm, D_MODEL), lambda i: (i, 0)), pl.BlockSpec((tm, D_MODEL), lambda i: (i, 1)),
                  pl.BlockSpec(memory_space=pl.ANY), pl.BlockSpec(memory_space=pl.ANY)],
        out_specs=pl.BlockSpec((tm, D_MODEL), lambda i: (i, 0)),
        out_shape=jax.ShapeDtypeStruct((N_ROWS, D_MODEL), BF16),
        scratch_shapes=[pltpu.VMEM((D_MODEL, D_MODEL), BF16), pltpu.VMEM((D_INNER, D_MODEL), BF16),
                        pltpu.VMEM((2, chunk, D_MODEL), F32), pltpu.SemaphoreType.DMA((2,))],
        compiler_params=_params(1, 52),
        name="branch_merge",
    )(yp_p, yp_s, ys_p, ys_s, gates, gates, w_bp, w_bs)


def _pool_prompt_kernel(u_ref, uprev_ref, pw_ref, ps_ref, o_ref):
    tiles_per_seq = SEQ // ROW_TILE
    lt = pl.program_id(0) % tiles_per_seq
    halo = POOL_BUF + 1
    t = lt * ROW_TILE + lax.broadcasted_iota(jnp.int32, (ROW_TILE, 1), 0)
    for g, w in enumerate(POOL_WINDOWS):
        lo = g * POOL_GROUP
        cur = u_ref[:, lo:lo + POOL_GROUP]
        s = jnp.concatenate([jnp.where(lt == 0, 0.0, uprev_ref[:, lo:lo + POOL_GROUP]), cur], axis=0)
        k = 1
        while k < w:
            s = s + pltpu.roll(s, k, axis=0)
            k *= 2
        s = s[halo:halo + ROW_TILE]
        cnt = jnp.minimum(w, t + 1).astype(F32)
        pooled = (s / cnt - cur).astype(BF16)
        mixed = jnp.dot(pooled, pw_ref[g].astype(BF16), preferred_element_type=F32)
        o_ref[:, lo:lo + POOL_GROUP] = (mixed * ps_ref[:, lo:lo + POOL_GROUP]).astype(BF16)


def _pool_prompt(u_pool, pool_w, pool_scale):
    halo = POOL_BUF + 1
    per = ROW_TILE // halo
    return pl.pallas_call(
        _pool_prompt_kernel,
        grid=(N_PROMPT // ROW_TILE,),
        in_specs=[pl.BlockSpec((ROW_TILE, D_MODEL), lambda i: (i, 0)),
                  pl.BlockSpec((halo, D_MODEL), lambda i: (jnp.maximum(i * per - 1, 0), 0)),
                  pl.BlockSpec((len(POOL_WINDOWS), POOL_GROUP, POOL_GROUP), lambda i: (0, 0, 0)),
                  pl.BlockSpec((1, D_MODEL), lambda i: (0, 0))],
        out_specs=pl.BlockSpec((ROW_TILE, D_MODEL), lambda i: (i, 0)),
        out_shape=jax.ShapeDtypeStruct((N_PROMPT, D_MODEL), BF16),
        compiler_params=_params(1, 48),
        name="pool_prompt",
    )(u_pool, u_pool, pool_w, pool_scale)


def _pool_sample_kernel(u_ref, buf_ref, pw_ref, ps_ref, o_ref):
    n_prev = min(PAST_LEN, POOL_BUF)
    for g, w in enumerate(POOL_WINDOWS):
        lo = g * POOL_GROUP

        def row(k):
            if k < POOL_BUF:
                return buf_ref[k, :, lo:lo + POOL_GROUP]
            return u_ref[k - POOL_BUF, :, lo:lo + POOL_GROUP]

        pooled = []
        for t in range(DEC_SEQ):
            cur = row(POOL_BUF + t)
            s = cur
            for j in range(1, w):
                s = s + row(POOL_BUF + t - j)
            cnt = float(min(w, t + 1 + n_prev))
            pooled.append(s / cnt - cur)
        pooled = jnp.concatenate(pooled, axis=0).astype(BF16)
        mixed = jnp.dot(pooled, pw_ref[g].astype(BF16), preferred_element_type=F32)
        y = (mixed * ps_ref[:, lo:lo + POOL_GROUP]).astype(BF16)
        for t in range(DEC_SEQ):
            o_ref[t, :, lo:lo + POOL_GROUP] = y[t * SAMPLE_BB:(t + 1) * SAMPLE_BB]


def _pool_sample(u_s, buf_t, pool_w, pool_scale):
    return pl.pallas_call(
        _pool_sample_kernel,
        grid=(DEC_BATCH // SAMPLE_BB,),
        in_specs=[pl.BlockSpec((DEC_SEQ, SAMPLE_BB, D_MODEL), lambda i: (SAMPLE_ROW_BLOCK, i, 0)),
                  pl.BlockSpec((POOL_BUF, SAMPLE_BB, D_MODEL), lambda i: (0, i, 0)),
                  pl.BlockSpec((len(POOL_WINDOWS), POOL_GROUP, POOL_GROUP), lambda i: (0, 0, 0)),
                  pl.BlockSpec((1, D_MODEL), lambda i: (0, 0))],
        out_specs=pl.BlockSpec((DEC_SEQ, SAMPLE_BB, D_MODEL), lambda i: (0, i, 0)),
        out_shape=jax.ShapeDtypeStruct((DEC_SEQ, DEC_BATCH, D_MODEL), BF16),
        compiler_params=_params(1, 40),
        name="pool_sample",
    )(u_s, buf_t, pool_w, pool_scale)


def _gated_group_norm(y, x, zs, dskip, norm):
    y = (y + x * dskip) * zs.astype(F32)
    return (_rms(y) * norm).astype(BF16)


def _ssd_prompt_step(c, xbc_ref, prev_ref, zs_ref, dt_ref, dtb_ref, alog_ref, cw_ref, cb_ref, dskip_ref, norm_ref,
                     y_ref, hout_ref, h_s, xa_ref):
    first = c == 0
    halo = CONV_HALO

    @pl.when(first)
    def _():
        h_s[...] = jnp.zeros_like(h_s)

    assert CONV_W == 4
    slab = 512
    for lo in range(0, CONV_DIM, slab):
        cols = slice(lo, lo + slab)
        ext = jnp.concatenate([jnp.where(first, 0.0, prev_ref[:, cols]), xbc_ref[:, cols]], axis=0)
        ext1 = pltpu.roll(ext, 1, axis=0)
        near = ext * cw_ref[3:4, cols] + ext1 * cw_ref[2:3, cols]
        far = ext * cw_ref[1:2, cols] + ext1 * cw_ref[0:1, cols]
        acc = near + pltpu.roll(far, 2, axis=0)
        xa_ref[:, cols] = _silu(acc[halo:halo + CHUNK] + cb_ref[:, cols]).astype(BF16)

    dt_t = jax.nn.softplus(dt_ref[...] + dtb_ref[...])
    la_t = dt_t * (-jnp.exp(alog_ref[...]))
    lane_t = lax.broadcasted_iota(jnp.int32, la_t.shape, 1)
    s_t = la_t
    k = 1
    while k < CHUNK:
        s_t = s_t + jnp.where(lane_t >= k, pltpu.roll(s_t, k, axis=1), 0.0)
        k *= 2
    log2e = 1.0 / jnp.log(2.0)
    s2_t = s_t * log2e
    r2_t = (s_t - jnp.log(dt_t)) * log2e
    tail2_t = jnp.exp2(s2_t[:, CHUNK - 1:CHUNK] - r2_t)

    tri = (lax.broadcasted_iota(jnp.int32, (CHUNK, CHUNK), 0)
           >= lax.broadcasted_iota(jnp.int32, (CHUNK, CHUNK), 1))
    lo_half = lax.broadcasted_iota(jnp.int32, (CHUNK, LANES), 1) < HEAD_DIM

    for g in range(N_GROUPS):
        x = xa_ref[:, g * GROUP_CH:(g + 1) * GROUP_CH].astype(F32)
        bm_b = xa_ref[:, XBC_B0 + g * D_STATE:XBC_B0 + (g + 1) * D_STATE]
        cm_b = xa_ref[:, XBC_C0 + g * D_STATE:XBC_C0 + (g + 1) * D_STATE]
        cb = lax.dot_general(cm_b, bm_b, (((1,), (1,)), ((), ())), preferred_element_type=F32)
        bm_t = bm_b.astype(F32).T
        h_old = h_s[g]
        inter = jnp.dot(cm_b, h_old.astype(BF16), preferred_element_type=F32)

        ys, hs = [], []
        for i in range(PAIRS_PER_GROUP):
            w_pair, bt_pair, col_pair = [], [], []
            for j in (g * HEADS_PER_GROUP + 2 * i, g * HEADS_PER_GROUP + 2 * i + 1):
                colb = jnp.broadcast_to(s2_t[j:j + 1, :], (CHUNK, CHUNK)).T
                expo = jnp.where(tri, colb - r2_t[j:j + 1, :], -jnp.inf)
                w_pair.append((cb * jnp.exp2(expo)).astype(BF16))
                bt_pair.append((bm_t * tail2_t[j:j + 1, :]).astype(BF16))
                col_pair.append(colb)
            xp = x[:, i * LANES:(i + 1) * LANES]
            rhs = jnp.concatenate([jnp.where(lo_half, xp, 0.0), jnp.where(lo_half, 0.0, xp)], axis=0).astype(BF16)
            lhs = jnp.concatenate([jnp.concatenate(w_pair, axis=1), jnp.concatenate(bt_pair, axis=1)], axis=0)
            out = jnp.dot(lhs, rhs, preferred_element_type=F32)
            e_sel = jnp.exp2(jnp.where(lo_half, col_pair[0], col_pair[1]))
            ys.append(out[0:CHUNK] + inter[:, i * LANES:(i + 1) * LANES] * e_sel)
            hs.append(h_old[:, i * LANES:(i + 1) * LANES] * e_sel[CHUNK - 1:CHUNK, :] + out[CHUNK:2 * CHUNK])
        h_s[g] = jnp.concatenate(hs, axis=1)
        cols = slice(g * GROUP_CH, (g + 1) * GROUP_CH)
        y_ref[:, cols] = _gated_group_norm(jnp.concatenate(ys, axis=1), x, zs_ref[:, cols],
                                           dskip_ref[:, cols], norm_ref[:, cols])

    @pl.when(c == N_CHUNKS - 1)
    def _():
        for g in range(N_GROUPS):
            for i in range(PAIRS_PER_GROUP):
                pair = g * PAIRS_PER_GROUP + i
                h_pair = h_s[g, :, i * LANES:(i + 1) * LANES].T
                hout_ref[0, 0, 2 * pair:2 * pair + 2] = h_pair.reshape(2, HEAD_DIM, D_STATE)


def _split3(v):
    hi = v.astype(BF16)
    r1 = v - hi.astype(F32)
    mid = r1.astype(BF16)
    lo = (r1 - mid.astype(F32)).astype(BF16)
    return hi, mid, lo


def _ssd_sample_step(g, xh_ref, xst_ref, bm_ref, bst_ref, cm_ref, cst_ref, zs_ref, dt_ref, dtb_ref, alog_ref,
                     cwx_ref, cbx_ref, cwb_ref, cbb_ref, cwc_ref, cbc_ref, dskip_ref, norm_ref, h0_ref,
                     y_ref, hout_ref):
    bb = SAMPLE_BB
    rows = DEC_SEQ * bb

    def conv_silu(cur_ref, st_ref, w_ref, b_ref):
        full = [st_ref[k] for k in range(CONV_W - 1)] + [cur_ref[t] for t in range(DEC_SEQ)]
        outs = []
        for t in range(DEC_SEQ):
            acc = b_ref[...] + full[t] * w_ref[0:1, :]
            for k in range(1, CONV_W):
                acc = acc + full[t + k] * w_ref[k:k + 1, :]
            outs.append(_silu(acc))
        return outs

    x = conv_silu(xh_ref, xst_ref, cwx_ref, cbx_ref)
    bm = conv_silu(bm_ref, bst_ref, cwb_ref, cbb_ref)
    cm = conv_silu(cm_ref, cst_ref, cwc_ref, cbc_ref)

    a = -jnp.exp(alog_ref[...])
    dt = [jax.nn.softplus(dt_ref[t] + dtb_ref[...]) for t in range(DEC_SEQ)]
    s = [dt[0] * a]
    for t in range(1, DEC_SEQ):
        s.append(s[t - 1] + dt[t] * a)
    sel = (lax.broadcasted_iota(jnp.int32, (N_HEADS, GROUP_CH), 0)
           == g * HEADS_PER_GROUP + lax.broadcasted_iota(jnp.int32, (N_HEADS, GROUP_CH), 1) // HEAD_DIM)
    sel = jnp.where(sel, 1.0, 0.0).astype(BF16)

    def expand(v):
        return sum(jnp.dot(p, sel, preferred_element_type=F32) for p in _split3(v))

    dt_e = expand(jnp.concatenate(dt, axis=0))
    s_e = expand(jnp.concatenate(s, axis=0))
    dt_e = [dt_e[t * bb:(t + 1) * bb] for t in range(DEC_SEQ)]
    s_e = [s_e[t * bb:(t + 1) * bb] for t in range(DEC_SEQ)]

    c_all = jnp.concatenate(cm, axis=0).astype(BF16)
    b_all = jnp.concatenate(bm, axis=0)
    last = DEC_SEQ - 1
    xw_all = jnp.concatenate([x[t] * dt_e[t] * jnp.exp(s_e[last] - s_e[t]) for t in range(DEC_SEQ)], axis=0)
    xw_t = xw_all.T.astype(BF16)
    dec_t = jnp.concatenate([jnp.exp(s_e[last])] + [jnp.zeros((bb, GROUP_CH), F32)] * last, axis=0).T
    row_b = lax.broadcasted_iota(jnp.int32, (rows, 1), 0) % bb
    inter = jnp.zeros((rows, GROUP_CH), F32)
    for b in range(bb):
        mine = row_b == b
        h0 = h0_ref[0, b].reshape(GROUP_CH, D_STATE)
        yb = lax.dot_general(c_all, h0.astype(BF16), (((1,), (1,)), ((), ())), preferred_element_type=F32)
        inter = inter + jnp.where(mine, yb, 0.0)
        dh = jnp.dot(xw_t, jnp.where(mine, b_all, 0.0).astype(BF16), preferred_element_type=F32)
        h_new = h0 * jnp.broadcast_to(dec_t[:, b:b + 1], (GROUP_CH, D_STATE)) + dh
        hout_ref[0, b] = h_new.reshape(HEADS_PER_GROUP, HEAD_DIM, D_STATE)

    for t in range(DEC_SEQ):
        y = inter[t * bb:(t + 1) * bb] * jnp.exp(s_e[t])
        for u in range(t + 1):
            cb = jnp.sum(cm[t] * bm[u], axis=-1, keepdims=True)
            y = y + cb * jnp.exp(s_e[t] - s_e[u]) * dt_e[u] * x[u]
        y_ref[t] = _gated_group_norm(y, x[t], zs_ref[t], dskip_ref[...], norm_ref[...])


N_PROMPT_SSD_IN = 10
N_SAMPLE_SSD_IN = 19
SSD_STEPS = BATCH * N_CHUNKS
SAMPLE_SSD_STEPS = (DEC_BATCH // SAMPLE_BB) * N_GROUPS
SAMPLE_EVERY = SSD_STEPS // SAMPLE_SSD_STEPS


def _ssd_kernel(*refs):
    p_in = refs[:N_PROMPT_SSD_IN]
    s_in = refs[N_PROMPT_SSD_IN:N_PROMPT_SSD_IN + N_SAMPLE_SSD_IN]
    yp_ref, hp_ref, ys_ref, hs_ref, h_s, xa_ref = refs[N_PROMPT_SSD_IN + N_SAMPLE_SSD_IN:]
    s = pl.program_id(0)
    _ssd_prompt_step(s % N_CHUNKS, *p_in, yp_ref, hp_ref, h_s, xa_ref)

    @pl.when(s % SAMPLE_EVERY == SAMPLE_EVERY - 1)
    def _():
        _ssd_sample_step((s // SAMPLE_EVERY) % N_GROUPS, *s_in, ys_ref, hs_ref)


def _ssd(xbc, zs, dt_t, dt_bias_col, a_log_col, xbc_s, conv_st, zs_s, dt_s, dt_bias_row, a_log_row,
         conv_w, conv_b, dskip_e, ssm_norm, h0):
    assert SSD_STEPS % SAMPLE_SSD_STEPS == 0
    bb = SAMPLE_BB
    per = CHUNK // CONV_HALO

    def full(rows, width):
        return pl.BlockSpec((rows, width), lambda s: (0, 0))

    prompt_specs = [pl.BlockSpec((CHUNK, CONV_DIM), lambda s: (s, 0)),
                    pl.BlockSpec((CONV_HALO, CONV_DIM), lambda s: (jnp.maximum(s * per - 1, 0), 0)),
                    pl.BlockSpec((CHUNK, D_INNER), lambda s: (s, 0)),
                    pl.BlockSpec((N_HEADS, CHUNK), lambda s: (0, s)),
                    full(N_HEADS, 1), full(N_HEADS, 1),
                    full(CONV_W, CONV_DIM), full(1, CONV_DIM), full(1, D_INNER), full(1, D_INNER)]

    def blk(s):
        return (s // SAMPLE_EVERY) // N_GROUPS

    def grp(s):
        return (s // SAMPLE_EVERY) % N_GROUPS

    def cur(width, col0):
        return pl.BlockSpec((DEC_SEQ, bb, width), lambda s: (SAMPLE_ROW_BLOCK, blk(s), col0 + grp(s)))

    def st(width, col0):
        return pl.BlockSpec((CONV_W - 1, bb, width), lambda s: (0, blk(s), col0 + grp(s)))

    def par(rows, width, col0):
        return pl.BlockSpec((rows, width), lambda s: (0, col0 + grp(s)))

    head_row = full(1, N_HEADS)
    state = pl.BlockSpec((1, bb, HEADS_PER_GROUP, HEAD_DIM, D_STATE), lambda s: (0, blk(s), grp(s), 0, 0))
    sample_specs = [cur(GROUP_CH, 0), st(GROUP_CH, 0),
                    cur(D_STATE, XBC_B_COL), st(D_STATE, XBC_B_COL),
                    cur(D_STATE, XBC_C_COL), st(D_STATE, XBC_C_COL),
                    cur(GROUP_CH, 0),
                    pl.BlockSpec((DEC_SEQ, bb, N_HEADS), lambda s: (SAMPLE_ROW_BLOCK, blk(s), 0)),
                    head_row, head_row,
                    par(CONV_W, GROUP_CH, 0), par(1, GROUP_CH, 0),
                    par(CONV_W, D_STATE, XBC_B_COL), par(1, D_STATE, XBC_B_COL),
                    par(CONV_W, D_STATE, XBC_C_COL), par(1, D_STATE, XBC_C_COL),
                    par(1, GROUP_CH, 0), par(1, GROUP_CH, 0), state]
    assert len(prompt_specs) == N_PROMPT_SSD_IN and len(sample_specs) == N_SAMPLE_SSD_IN
    return pl.pallas_call(
        _ssd_kernel,
        grid=(SSD_STEPS,),
        in_specs=prompt_specs + sample_specs,
        out_specs=[pl.BlockSpec((CHUNK, D_INNER), lambda s: (s, 0)),
                   pl.BlockSpec((1, 1, N_HEADS, HEAD_DIM, D_STATE), lambda s: (0, s // N_CHUNKS, 0, 0, 0)),
                   pl.BlockSpec((DEC_SEQ, bb, GROUP_CH), lambda s: (0, blk(s), grp(s))),
                   state],
        out_shape=[jax.ShapeDtypeStruct((N_PROMPT, D_INNER), BF16),
                   jax.ShapeDtypeStruct((1, BATCH, N_HEADS, HEAD_DIM, D_STATE), F32),
                   jax.ShapeDtypeStruct((DEC_SEQ, DEC_BATCH, D_INNER), BF16),
                   jax.ShapeDtypeStruct((1, DEC_BATCH, N_HEADS, HEAD_DIM, D_STATE), F32)],
        scratch_shapes=[pltpu.VMEM((N_GROUPS, D_STATE, GROUP_CH), F32), pltpu.VMEM((CHUNK, CONV_DIM), BF16)],
        compiler_params=_params(1, 60),
        name="ssd_mixers",
    )(xbc, xbc, zs, dt_t, dt_bias_col, a_log_col, conv_w, conv_b, dskip_e, ssm_norm,
      xbc_s, conv_st, xbc_s, conv_st, xbc_s, conv_st, zs_s, dt_s, dt_bias_row, a_log_row,
      conv_w, conv_b, conv_w, conv_b, conv_w, conv_b, dskip_e, ssm_norm, h0)


def kernel(x_prompt, x_sample, c_prompt, c_sample, state_ssm, state_conv, state_pool, w_ada, b_ada, norm_ffn1,
           w13_ffn1, w2_ffn1, norm_mix, w_in, pool_w, pool_scale, conv_w, conv_b, dt_bias, a_log, d_skip,
           ssm_norm, w_branch_pool, w_branch_ssm, w_out, norm_ffn2, w13_ffn2, w2_ffn2, norm_final):
    d = D_MODEL

    def layer0(w):
        return w.reshape(w.shape[1:])

    def row(v):
        return v.reshape(1, -1)

    xp = x_prompt.reshape(N_PROMPT, d)
    xs = x_sample.transpose(1, 0, 2).reshape(N_SAMPLE, d)

    n_c = BATCH + DEC_BATCH
    c_all = jnp.pad(jnp.concatenate([c_prompt, c_sample], axis=0), ((0, -n_c % 8), (0, 0)))
    mods = _ada(c_all, layer0(w_ada), b_ada.reshape(1, N_MOD * d))
    mods_p = mods[:BATCH]
    mods_s = mods[BATCH:n_c]

    a1 = _normmod(xp, xs, row(norm_ffn1), mods_p, mods_s, 0, 1)
    act1 = _ffn_up(a1, layer0(w13_ffn1))
    h1, u = _res_block(act1, layer0(w2_ffn1), (xp, xs), mods_p, mods_s, 2, 0.5, row(norm_mix), (3, 4),
                       tm=256, chunk=176, vmem_mib=58)

    w_in_t = layer0(w_in).T
    col_z, col_xbc, col_dt = d, d + D_INNER, d + D_INNER + CONV_DIM
    conv_buf = layer0(state_conv)
    u_pool = _in_proj(u, w_in_t, 0, d, 1024, F32)
    zs = _in_proj(u, w_in_t, col_z, D_INNER, 1024, BF16, act=_silu)
    conv_w0 = layer0(conv_w)
    conv_b0 = row(conv_b)
    xbc = _in_proj(u, w_in_t, col_xbc, CONV_DIM, 1024, F32)
    dt_raw = _in_proj(u, w_in_t, col_dt, N_HEADS, N_HEADS, F32)
    gates = _in_proj(u, w_in_t, col_dt + N_HEADS, 2 * d, 1024, BF16, act=_sigmoid)

    pw = layer0(pool_w)
    ps = row(pool_scale)
    u_pool_s = u_pool[N_PROMPT:].reshape(DEC_SEQ, DEC_BATCH, d)
    pool_buf = layer0(state_pool)
    y_pool_p = _pool_prompt(u_pool, pw, ps)
    y_pool_s = _pool_sample(u_pool.reshape(ROW_SLABS, DEC_BATCH, d), pool_buf.transpose(1, 0, 2),
                            pw, ps).reshape(N_SAMPLE, d)

    dskip_e = row(jnp.repeat(d_skip.reshape(N_HEADS), HEAD_DIM))
    norm_row = row(ssm_norm)
    y_ssm_p, prompt_ssm, y_ssm_s, sample_ssm = _ssd(
        xbc, zs, dt_raw[:N_PROMPT].T, dt_bias.reshape(N_HEADS, 1), a_log.reshape(N_HEADS, 1),
        xbc.reshape(ROW_SLABS, DEC_BATCH, CONV_DIM), conv_buf.transpose(1, 0, 2),
        zs.reshape(ROW_SLABS, DEC_BATCH, D_INNER), dt_raw.reshape(ROW_SLABS, DEC_BATCH, N_HEADS),
        row(dt_bias), row(a_log), conv_w0, conv_b0, dskip_e, norm_row, state_ssm)

    merged = _merge(y_pool_p, y_pool_s, y_ssm_p, y_ssm_s.reshape(N_SAMPLE, D_INNER), gates,
                    layer0(w_branch_pool), layer0(w_branch_ssm))
    h2, a3 = _res_block(merged, layer0(w_out), h1, mods_p, mods_s, 5, 1.0, row(norm_ffn2), (6, 7),
                        tm=512, chunk=256, vmem_mib=52)
    act2 = _ffn_up(a3, layer0(w13_ffn2))
    y_p, y_s = _res_block(act2, layer0(w2_ffn2), h2, mods_p, mods_s, 8, 0.5, row(norm_final), None,
                          tm=256, chunk=176, vmem_mib=56)

    y_prompt = y_p.reshape(BATCH, SEQ, d)
    y_sample = y_s.reshape(DEC_SEQ, DEC_BATCH, d).transpose(1, 0, 2)
    keep = CONV_W - 1
    seq_end = (jnp.arange(BATCH) + 1) * SEQ
    conv_rows = (seq_end[:, None] - keep + jnp.arange(keep)[None, :]).reshape(-1)
    pool_rows = (seq_end[:, None] - POOL_BUF + jnp.arange(POOL_BUF)[None, :]).reshape(-1)
    prompt_conv = jnp.take(xbc, conv_rows, axis=0).reshape(1, BATCH, keep, CONV_DIM)
    prompt_pool = jnp.take(u_pool, pool_rows, axis=0).reshape(1, BATCH, POOL_BUF, d)
    assert DEC_SEQ >= keep
    xbc_last = xbc[N_ROWS - keep * DEC_BATCH:].reshape(keep, DEC_BATCH, CONV_DIM)
    sample_conv = xbc_last.transpose(1, 0, 2)[None]
    sample_pool = jnp.concatenate([pool_buf, u_pool_s.transpose(1, 0, 2)], axis=1)[:, -POOL_BUF:][None]
    return (y_prompt, y_sample, prompt_ssm, prompt_conv, prompt_pool, sample_ssm, sample_conv, sample_pool)
```

```python
import functools

import jax
import jax.numpy as jnp
from jax import lax
from jax.experimental import pallas as pl
from jax.experimental.pallas import tpu as pltpu

F32 = jnp.float32
BF16 = jnp.bfloat16

D_MODEL = 2048
BATCH = 4
SEQ = 2048
DEC_BATCH = 128
DEC_SEQ = 4
PAST_LEN = 16384
POOL_WINDOWS = (2, 4, 8, 16)
POOL_GROUP = D_MODEL // len(POOL_WINDOWS)
POOL_BUF = max(POOL_WINDOWS) - 1
D_INNER = 2 * D_MODEL
HEAD_DIM = 64
N_HEADS = D_INNER // HEAD_DIM
D_STATE = 128
N_GROUPS = 8
HEADS_PER_GROUP = N_HEADS // N_GROUPS
GROUP_CH = D_INNER // N_GROUPS
CONV_W = 4
CONV_DIM = D_INNER + 2 * N_GROUPS * D_STATE
CHUNK = 128
D_FF = 256 * ((8 * D_MODEL // 3 + 255) // 256)
N_MOD = 9
EPS = 1e-6

N_PROMPT = BATCH * SEQ
N_SAMPLE = DEC_BATCH * DEC_SEQ
N_ROWS = N_PROMPT + N_SAMPLE
ROW_TILE = 512
N_ROW_TILES = N_ROWS // ROW_TILE
BIG_TILE = 1024
FULL_BIG_TILES = N_ROWS // BIG_TILE
REM_ROWS = N_ROWS - FULL_BIG_TILES * BIG_TILE
N_BIG_TILES = FULL_BIG_TILES + (1 if REM_ROWS else 0)
N_CHUNKS = SEQ // CHUNK
LANES = 128
PAIRS_PER_GROUP = HEADS_PER_GROUP // 2
SAMPLE_BB = 32
ROW_SLABS = N_ROWS // DEC_BATCH
SAMPLE_ROW_BLOCK = N_PROMPT // (DEC_BATCH * DEC_SEQ)
XBC_B0 = D_INNER
XBC_C0 = D_INNER + N_GROUPS * D_STATE
XBC_B_COL = XBC_B0 // D_STATE
XBC_C_COL = XBC_C0 // D_STATE
CONV_HALO = 8
MIB = 1024 * 1024


def _params(n_axes, vmem_mib):
    return pltpu.CompilerParams(dimension_semantics=("arbitrary",) * n_axes,
                                vmem_limit_bytes=vmem_mib * MIB)


def _sigmoid(x):
    return 0.5 * jnp.tanh(0.5 * x) + 0.5


def _silu(x):
    return x * _sigmoid(x)


def _spec_p(tm, width, col=0):
    last = N_PROMPT // tm - 1
    return pl.BlockSpec((tm, width), lambda i: (jnp.minimum(i, last), col))


def _spec_s(tm, width, col=0):
    first = N_PROMPT // tm
    return pl.BlockSpec((tm, width), lambda i: (jnp.maximum(i - first, 0), col))


def _mod_specs(chunk):
    return [pl.BlockSpec((BATCH, D_MODEL), lambda i: (0, chunk)),
            pl.BlockSpec((DEC_BATCH, D_MODEL), lambda i: (0, chunk))]


def _rms(x):
    return x * lax.rsqrt(jnp.mean(x * x, axis=-1, keepdims=True) + EPS)


def _ada_kernel(c_ref, w_ref, b_ref, o_ref):
    a = _silu(c_ref[...]).astype(BF16)
    o_ref[...] = jnp.dot(a, w_ref[...].astype(BF16), preferred_element_type=F32) + b_ref[...]


def _ada(c_all, w, b):
    m = c_all.shape[0]
    n = w.shape[1]
    tn = 1024
    return pl.pallas_call(
        _ada_kernel,
        grid=(n // tn,),
        in_specs=[pl.BlockSpec((m, D_MODEL), lambda j: (0, 0)),
                  pl.BlockSpec((D_MODEL, tn), lambda j: (0, j)),
                  pl.BlockSpec((1, tn), lambda j: (0, j))],
        out_specs=pl.BlockSpec((m, tn), lambda j: (0, j)),
        out_shape=jax.ShapeDtypeStruct((m, n), F32),
        compiler_params=_params(1, 40),
        name="ada_mods",
    )(c_all, w, b)


def _normmod_kernel(xp_ref, xs_ref, g_ref, shp_ref, shs_ref, scp_ref, scs_ref, o_ref):
    i = pl.program_id(0)
    sub = 64

    def run(x_ref, shift_of, scale_of):
        for r in range(0, ROW_TILE, sub):
            y = _rms(x_ref[r:r + sub, :]) * g_ref[...]
            o_ref[r:r + sub, :] = (y * (1.0 + scale_of(r)) + shift_of(r)).astype(BF16)

    @pl.when(i < N_PROMPT // ROW_TILE)
    def _():
        seq = i // (SEQ // ROW_TILE)
        shift = shp_ref[pl.ds(seq, 1), :]
        scale = scp_ref[pl.ds(seq, 1), :]
        run(xp_ref, lambda r: shift, lambda r: scale)

    @pl.when(i >= N_PROMPT // ROW_TILE)
    def _():
        def rows(ref):
            return lambda r: ref[r % DEC_BATCH:r % DEC_BATCH + sub, :]

        run(xs_ref, rows(shs_ref), rows(scs_ref))


def _normmod(xp, xs, g, mods_p, mods_s, shift_chunk, scale_chunk):
    return pl.pallas_call(
        _normmod_kernel,
        grid=(N_ROW_TILES,),
        in_specs=[_spec_p(ROW_TILE, D_MODEL), _spec_s(ROW_TILE, D_MODEL),
                  pl.BlockSpec((1, D_MODEL), lambda i: (0, 0))]
                 + _mod_specs(shift_chunk) + _mod_specs(scale_chunk),
        out_specs=pl.BlockSpec((ROW_TILE, D_MODEL), lambda i: (i, 0)),
        out_shape=jax.ShapeDtypeStruct((N_ROWS, D_MODEL), BF16),
        compiler_params=_params(1, 48),
        name="norm_modulate",
    )(xp, xs, g, mods_p, mods_s, mods_p, mods_s)


def _per_row_tile(i, body):
    @pl.when(i < FULL_BIG_TILES)
    def _():
        body(BIG_TILE)

    @pl.when(i == FULL_BIG_TILES)
    def _():
        body(REM_ROWS)


def _ffn_up_kernel(a_ref, wa_ref, wb_ref, o_ref, wa_s, wb_s):
    i = pl.program_id(1)

    @pl.when(i == 0)
    def _():
        wa_s[...] = wa_ref[...].astype(BF16)
        wb_s[...] = wb_ref[...].astype(BF16)

    def body(rows):
        a = a_ref[0:rows, :]
        ha = jnp.dot(a, wa_s[...], preferred_element_type=F32)
        hb = jnp.dot(a, wb_s[...], preferred_element_type=F32)
        o_ref[0:rows, :] = (_silu(ha) * hb).astype(BF16)

    _per_row_tile(i, body)


def _ffn_up(a, w13):
    tn = 512
    nt = D_FF // tn
    return pl.pallas_call(
        _ffn_up_kernel,
        grid=(nt, N_BIG_TILES),
        in_specs=[pl.BlockSpec((BIG_TILE, D_MODEL), lambda j, i: (i, 0)),
                  pl.BlockSpec((D_MODEL, tn), lambda j, i: (0, j)),
                  pl.BlockSpec((D_MODEL, tn), lambda j, i: (0, nt + j))],
        out_specs=pl.BlockSpec((BIG_TILE, tn), lambda j, i: (i, j)),
        out_shape=jax.ShapeDtypeStruct((N_ROWS, D_FF), BF16),
        scratch_shapes=[pltpu.VMEM((D_MODEL, tn), BF16), pltpu.VMEM((D_MODEL, tn), BF16)],
        compiler_params=_params(2, 56),
        name="ffn_up",
    )(a, w13, w13)


IN_PROJ_TN = 1024
IN_PROJ_WIDTHS = (D_MODEL, D_INNER, CONV_DIM, N_HEADS, 2 * D_MODEL)
IN_PROJ_TILES = tuple(max(w // IN_PROJ_TN, 1) for w in IN_PROJ_WIDTHS)
IN_PROJ_START = tuple(sum(IN_PROJ_TILES[:p]) for p in range(len(IN_PROJ_TILES)))
IN_PROJ_DT_PHASE = 3


def _in_proj_kernel(a_ref, wt_ref, wdt_ref, up_ref, zs_ref, xbc_ref, dt_ref, gates_ref, wt_s, wdt_s):
    j = pl.program_id(0)
    i = pl.program_id(1)
    dt_j = IN_PROJ_START[IN_PROJ_DT_PHASE]

    @pl.when(jnp.logical_and(i == 0, j != dt_j))
    def _():
        wt_s[...] = wt_ref[...].astype(BF16)

    @pl.when(jnp.logical_and(i == 0, j == dt_j))
    def _():
        wdt_s[...] = wdt_ref[...].astype(BF16)

    def phase(p, o_ref, w_s, act):
        lo = IN_PROJ_START[p]

        @pl.when(jnp.logical_and(j >= lo, j < lo + IN_PROJ_TILES[p]))
        def _():
            def body(rows):
                acc = lax.dot_general(a_ref[0:rows, :], w_s[...], (((1,), (1,)), ((), ())),
                                      preferred_element_type=F32)
                if act is not None:
                    acc = act(acc)
                o_ref[0:rows, :] = acc.astype(o_ref.dtype)

            _per_row_tile(i, body)

    phase(0, up_ref, wt_s, None)
    phase(1, zs_ref, wt_s, _silu)
    phase(2, xbc_ref, wt_s, None)
    phase(3, dt_ref, wdt_s, None)
    phase(4, gates_ref, wt_s, _sigmoid)


def _in_proj(a, wt):
    k = a.shape[1]
    tn = IN_PROJ_TN
    dt_j = IN_PROJ_START[IN_PROJ_DT_PHASE]
    row0 = [sum(IN_PROJ_WIDTHS[:p]) for p in range(len(IN_PROJ_WIDTHS))]
    assert all(r % 8 == 0 for r in row0) and row0[IN_PROJ_DT_PHASE] % N_HEADS == 0

    def w_row(j, i):
        before_dt = jnp.minimum(j, dt_j - 1) * tn
        after_dt = row0[IN_PROJ_DT_PHASE + 1] + (j - dt_j - 1) * tn
        return pl.multiple_of(jnp.where(j > dt_j, after_dt, before_dt), 8), 0

    def out_spec(p, width):
        lo, n = IN_PROJ_START[p], IN_PROJ_TILES[p]

        def index(j, i):
            row = jnp.where(j < lo, 0, jnp.where(j >= lo + n, N_BIG_TILES - 1, i))
            return row, jnp.clip(j - lo, 0, n - 1)

        return pl.BlockSpec((BIG_TILE, width), index)

    dtypes = (F32, BF16, F32, F32, BF16)
    return pl.pallas_call(
        _in_proj_kernel,
        grid=(sum(IN_PROJ_TILES), N_BIG_TILES),
        in_specs=[pl.BlockSpec((BIG_TILE, k), lambda j, i: (i, 0)),
                  pl.BlockSpec((pl.Element(tn), pl.Element(k)), w_row),
                  pl.BlockSpec((N_HEADS, k), lambda j, i: (row0[IN_PROJ_DT_PHASE] // N_HEADS, 0))],
        out_specs=[out_spec(p, min(w, tn)) for p, w in enumerate(IN_PROJ_WIDTHS)],
        out_shape=[jax.ShapeDtypeStruct((N_ROWS, w), dt) for w, dt in zip(IN_PROJ_WIDTHS, dtypes)],
        scratch_shapes=[pltpu.VMEM((tn, k), BF16), pltpu.VMEM((N_HEADS, k), BF16)],
        compiler_params=_params(2, 60),
        name="in_proj",
    )(a, wt, wt)


def _load_weight(w_hbm, w_s, stage, sem, chunk):
    n = w_hbm.shape[0] // chunk

    def copy(c):
        return pltpu.make_async_copy(w_hbm.at[pl.ds(c * chunk, chunk), :], stage.at[c % 2], sem.at[c % 2])

    copy(0).start()
    for c in range(n):
        if c + 1 < n:
            copy(c + 1).start()
        copy(c).wait()
        w_s[c * chunk:(c + 1) * chunk, :] = stage[c % 2].astype(BF16)


def _res_block_kernel(*refs, tm, scale, two_source_res, final, chunk):
    refs = list(refs)
    a_ref, w_hbm = refs[:2]
    pos = 2
    if two_source_res:
        resp_ref, ress_ref = refs[pos:pos + 2]
        pos += 2
    else:
        res_ref = refs[pos]
        pos += 1
    gp_ref, gs_ref, g_ref = refs[pos:pos + 3]
    pos += 3
    if not final:
        shp_ref, shs_ref, scp_ref, scs_ref = refs[pos:pos + 4]
        pos += 4
    outs = refs[pos:pos + 2]
    w_s, stage, sem = refs[pos + 2:]
    i = pl.program_id(0)

    @pl.when(i == 0)
    def _():
        _load_weight(w_hbm, w_s, stage, sem, chunk)

    def tile(res_ref_, mods_of, out_ref):
        gate = mods_of(gp_ref, gs_ref)
        if scale != 1.0:
            gate = scale * gate
        h = res_ref_[...] + gate * jnp.dot(a_ref[...], w_s[...], preferred_element_type=F32)
        y = _rms(h)
        if final:
            out_ref[...] = y * g_ref[...]
        else:
            out_ref[...] = h
            gain = g_ref[...] * (1.0 + mods_of(scp_ref, scs_ref))
            outs[1][...] = (y * gain + mods_of(shp_ref, shs_ref)).astype(BF16)

    @pl.when(i < N_PROMPT // tm)
    def _():
        seq = i // (SEQ // tm)
        tile(resp_ref if two_source_res else res_ref, lambda p, s: p[pl.ds(seq, 1), :], outs[0])

    @pl.when(i >= N_PROMPT // tm)
    def _():
        tile(ress_ref if two_source_res else res_ref,
             lambda p, s: jnp.concatenate([s[...]] * (tm // DEC_BATCH), axis=0), outs[1] if final else outs[0])


def _res_block(a, w, res, mods_p, mods_s, gate_chunk, scale, norm_g, next_chunks, tm, chunk, vmem_mib):
    k = a.shape[1]
    final = next_chunks is None
    two = isinstance(res, tuple)
    row = pl.BlockSpec((tm, D_MODEL), lambda i: (i, 0))
    in_specs = [pl.BlockSpec((tm, k), lambda i: (i, 0)), pl.BlockSpec(memory_space=pl.ANY)]
    args = [a, w]
    if two:
        in_specs += [_spec_p(tm, D_MODEL), _spec_s(tm, D_MODEL)]
        args += list(res)
    else:
        in_specs.append(row)
        args.append(res)
    in_specs += _mod_specs(gate_chunk) + [pl.BlockSpec((1, D_MODEL), lambda i: (0, 0))]
    args += [mods_p, mods_s, norm_g]
    if final:
        out_specs = [_spec_p(tm, D_MODEL), _spec_s(tm, D_MODEL)]
        out_shape = [jax.ShapeDtypeStruct((N_PROMPT, D_MODEL), F32), jax.ShapeDtypeStruct((N_SAMPLE, D_MODEL), F32)]
    else:
        in_specs += _mod_specs(next_chunks[0]) + _mod_specs(next_chunks[1])
        args += [mods_p, mods_s, mods_p, mods_s]
        out_specs = [row, row]
        out_shape = [jax.ShapeDtypeStruct((N_ROWS, D_MODEL), F32), jax.ShapeDtypeStruct((N_ROWS, D_MODEL), BF16)]
    return pl.pallas_call(
        functools.partial(_res_block_kernel, tm=tm, scale=scale, two_source_res=two, final=final, chunk=chunk),
        grid=(N_ROWS // tm,),
        in_specs=in_specs,
        out_specs=out_specs,
        out_shape=out_shape,
        scratch_shapes=[pltpu.VMEM((k, D_MODEL), BF16), pltpu.VMEM((2, chunk, D_MODEL), F32),
                        pltpu.SemaphoreType.DMA((2,))],
        compiler_params=_params(1, vmem_mib),
        name="res_block_final" if final else "res_block",
    )(*args)


def _merge_kernel(ypp_ref, yps_ref, ysp_ref, yss_ref, gp_ref, gs_ref, wp_hbm, ws_hbm, o_ref,
                  wp_s, ws_s, stage, sem, *, tm, chunk):
    i = pl.program_id(0)

    @pl.when(i == 0)
    def _():
        _load_weight(wp_hbm, wp_s, stage, sem, chunk)
        _load_weight(ws_hbm, ws_s, stage, sem, chunk)

    def tile(yp_ref, ys_ref):
        mp = jnp.dot(yp_ref[...], wp_s[...], preferred_element_type=F32)
        ms = jnp.dot(ys_ref[...], ws_s[...], preferred_element_type=F32)
        o_ref[...] = (gp_ref[...].astype(F32) * mp + gs_ref[...].astype(F32) * ms).astype(BF16)

    @pl.when(i < N_PROMPT // tm)
    def _():
        tile(ypp_ref, ysp_ref)

    @pl.when(i >= N_PROMPT // tm)
    def _():
        tile(yps_ref, yss_ref)


def _merge(yp_p, yp_s, ys_p, ys_s, gates, w_bp, w_bs):
    tm, chunk = 256, 256
    return pl.pallas_call(
        functools.partial(_merge_kernel, tm=tm, chunk=chunk),
        grid=(N_ROWS // tm,),
        in_specs=[_spec_p(tm, D_MODEL), _spec_s(tm, D_MODEL), _spec_p(tm, D_INNER), _spec_s(tm, D_INNER),
                  pl.BlockSpec((tm, D_MODEL), lambda i: (i, 0)), pl.BlockSpec((tm, D_MODEL), lambda i: (i, 1)),
                  pl.BlockSpec(memory_space=pl.ANY), pl.BlockSpec(memory_space=pl.ANY)],
        out_specs=pl.BlockSpec((tm, D_MODEL), lambda i: (i, 0)),
        out_shape=jax.ShapeDtypeStruct((N_ROWS, D_MODEL), BF16),
        scratch_shapes=[pltpu.VMEM((D_MODEL, D_MODEL), BF16), pltpu.VMEM((D_INNER, D_MODEL), BF16),
                        pltpu.VMEM((2, chunk, D_MODEL), F32), pltpu.SemaphoreType.DMA((2,))],
        compiler_params=_params(1, 52),
        name="branch_merge",
    )(yp_p, yp_s, ys_p, ys_s, gates, gates, w_bp, w_bs)


def _pool_prompt_kernel(u_ref, uprev_ref, pw_ref, ps_ref, o_ref):
    tiles_per_seq = SEQ // ROW_TILE
    lt = pl.program_id(0) % tiles_per_seq
    halo = POOL_BUF + 1
    t = lt * ROW_TILE + lax.broadcasted_iota(jnp.int32, (ROW_TILE, 1), 0)
    for g, w in enumerate(POOL_WINDOWS):
        lo = g * POOL_GROUP
        cur = u_ref[:, lo:lo + POOL_GROUP]
        s = jnp.concatenate([jnp.where(lt == 0, 0.0, uprev_ref[:, lo:lo + POOL_GROUP]), cur], axis=0)
        k = 1
        while k < w:
            s = s + pltpu.roll(s, k, axis=0)
            k *= 2
        s = s[halo:halo + ROW_TILE]
        cnt = jnp.minimum(w, t + 1).astype(F32)
        pooled = (s / cnt - cur).astype(BF16)
        mixed = jnp.dot(pooled, pw_ref[g].astype(BF16), preferred_element_type=F32)
        o_ref[:, lo:lo + POOL_GROUP] = (mixed * ps_ref[:, lo:lo + POOL_GROUP]).astype(BF16)


def _pool_prompt(u_pool, pool_w, pool_scale):
    halo = POOL_BUF + 1
    per = ROW_TILE // halo
    return pl.pallas_call(
        _pool_prompt_kernel,
        grid=(N_PROMPT // ROW_TILE,),
        in_specs=[pl.BlockSpec((ROW_TILE, D_MODEL), lambda i: (i, 0)),
                  pl.BlockSpec((halo, D_MODEL), lambda i: (jnp.maximum(i * per - 1, 0), 0)),
                  pl.BlockSpec((len(POOL_WINDOWS), POOL_GROUP, POOL_GROUP), lambda i: (0, 0, 0)),
                  pl.BlockSpec((1, D_MODEL), lambda i: (0, 0))],
        out_specs=pl.BlockSpec((ROW_TILE, D_MODEL), lambda i: (i, 0)),
        out_shape=jax.ShapeDtypeStruct((N_PROMPT, D_MODEL), BF16),
        compiler_params=_params(1, 48),
        name="pool_prompt",
    )(u_pool, u_pool, pool_w, pool_scale)


def _pool_sample_kernel(u_ref, buf_ref, pw_ref, ps_ref, o_ref):
    n_prev = min(PAST_LEN, POOL_BUF)
    for g, w in enumerate(POOL_WINDOWS):
        lo = g * POOL_GROUP

        def row(k):
            if k < POOL_BUF:
                return buf_ref[k, :, lo:lo + POOL_GROUP]
            return u_ref[k - POOL_BUF, :, lo:lo + POOL_GROUP]

        pooled = []
        for t in range(DEC_SEQ):
            cur = row(POOL_BUF + t)
            s = cur
            for j in range(1, w):
                s = s + row(POOL_BUF + t - j)
            cnt = float(min(w, t + 1 + n_prev))
            pooled.append(s / cnt - cur)
        pooled = jnp.concatenate(pooled, axis=0).astype(BF16)
        mixed = jnp.dot(pooled, pw_ref[g].astype(BF16), preferred_element_type=F32)
        y = (mixed * ps_ref[:, lo:lo + POOL_GROUP]).astype(BF16)
        for t in range(DEC_SEQ):
            o_ref[t, :, lo:lo + POOL_GROUP] = y[t * SAMPLE_BB:(t + 1) * SAMPLE_BB]


def _pool_sample(u_s, buf_t, pool_w, pool_scale):
    return pl.pallas_call(
        _pool_sample_kernel,
        grid=(DEC_BATCH // SAMPLE_BB,),
        in_specs=[pl.BlockSpec((DEC_SEQ, SAMPLE_BB, D_MODEL), lambda i: (SAMPLE_ROW_BLOCK, i, 0)),
                  pl.BlockSpec((POOL_BUF, SAMPLE_BB, D_MODEL), lambda i: (0, i, 0)),
                  pl.BlockSpec((len(POOL_WINDOWS), POOL_GROUP, POOL_GROUP), lambda i: (0, 0, 0)),
                  pl.BlockSpec((1, D_MODEL), lambda i: (0, 0))],
        out_specs=pl.BlockSpec((DEC_SEQ, SAMPLE_BB, D_MODEL), lambda i: (0, i, 0)),
        out_shape=jax.ShapeDtypeStruct((DEC_SEQ, DEC_BATCH, D_MODEL), BF16),
        compiler_params=_params(1, 40),
        name="pool_sample",
    )(u_s, buf_t, pool_w, pool_scale)


def _gated_group_norm(y, x, zs, dskip, norm):
    y = (y + x * dskip) * zs.astype(F32)
    return (_rms(y) * norm).astype(BF16)


def _ssd_prompt_step(c, xbc_ref, prev_ref, zs_ref, dt_ref, dtb_ref, alog_ref, cw_ref, cb_ref, dskip_ref, norm_ref,
                     y_ref, hout_ref, h_s, xa_ref):
    first = c == 0
    halo = CONV_HALO

    @pl.when(first)
    def _():
        h_s[...] = jnp.zeros_like(h_s)

    assert CONV_W == 4
    slab = 512
    for lo in range(0, CONV_DIM, slab):
        cols = slice(lo, lo + slab)
        ext = jnp.concatenate([jnp.where(first, 0.0, prev_ref[:, cols]), xbc_ref[:, cols]], axis=0)
        ext1 = pltpu.roll(ext, 1, axis=0)
        near = ext * cw_ref[3:4, cols] + ext1 * cw_ref[2:3, cols]
        far = ext * cw_ref[1:2, cols] + ext1 * cw_ref[0:1, cols]
        acc = near + pltpu.roll(far, 2, axis=0)
        xa_ref[:, cols] = _silu(acc[halo:halo + CHUNK] + cb_ref[:, cols]).astype(BF16)

    dt_t = jax.nn.softplus(dt_ref[...] + dtb_ref[...])
    la_t = dt_t * (-jnp.exp(alog_ref[...]))
    lane_t = lax.broadcasted_iota(jnp.int32, la_t.shape, 1)
    s_t = la_t
    k = 1
    while k < CHUNK:
        s_t = s_t + jnp.where(lane_t >= k, pltpu.roll(s_t, k, axis=1), 0.0)
        k *= 2
    log2e = 1.0 / jnp.log(2.0)
    s2_t = s_t * log2e
    r2_t = (s_t - jnp.log(dt_t)) * log2e
    tail2_t = jnp.exp2(s2_t[:, CHUNK - 1:CHUNK] - r2_t)

    tri = (lax.broadcasted_iota(jnp.int32, (CHUNK, CHUNK), 0)
           >= lax.broadcasted_iota(jnp.int32, (CHUNK, CHUNK), 1))
    lo_half = lax.broadcasted_iota(jnp.int32, (CHUNK, LANES), 1) < HEAD_DIM

    for g in range(N_GROUPS):
        x = xa_ref[:, g * GROUP_CH:(g + 1) * GROUP_CH].astype(F32)
        bm_b = xa_ref[:, XBC_B0 + g * D_STATE:XBC_B0 + (g + 1) * D_STATE]
        cm_b = xa_ref[:, XBC_C0 + g * D_STATE:XBC_C0 + (g + 1) * D_STATE]
        cb = lax.dot_general(cm_b, bm_b, (((1,), (1,)), ((), ())), preferred_element_type=F32)
        bm_t = bm_b.astype(F32).T
        h_old = h_s[g]
        inter = jnp.dot(cm_b, h_old.astype(BF16), preferred_element_type=F32)

        ys, hs = [], []
        for i in range(PAIRS_PER_GROUP):
            w_pair, bt_pair, col_pair = [], [], []
            for j in (g * HEADS_PER_GROUP + 2 * i, g * HEADS_PER_GROUP + 2 * i + 1):
                colb = jnp.broadcast_to(s2_t[j:j + 1, :], (CHUNK, CHUNK)).T
                expo = jnp.where(tri, colb - r2_t[j:j + 1, :], -jnp.inf)
                w_pair.append((cb * jnp.exp2(expo)).astype(BF16))
                bt_pair.append((bm_t * tail2_t[j:j + 1, :]).astype(BF16))
                col_pair.append(colb)
            xp = x[:, i * LANES:(i + 1) * LANES]
            rhs = jnp.concatenate([jnp.where(lo_half, xp, 0.0), jnp.where(lo_half, 0.0, xp)], axis=0).astype(BF16)
            lhs = jnp.concatenate([jnp.concatenate(w_pair, axis=1), jnp.concatenate(bt_pair, axis=1)], axis=0)
            out = jnp.dot(lhs, rhs, preferred_element_type=F32)
            e_sel = jnp.exp2(jnp.where(lo_half, col_pair[0], col_pair[1]))
            ys.append(out[0:CHUNK] + inter[:, i * LANES:(i + 1) * LANES] * e_sel)
            hs.append(h_old[:, i * LANES:(i + 1) * LANES] * e_sel[CHUNK - 1:CHUNK, :] + out[CHUNK:2 * CHUNK])
        h_s[g] = jnp.concatenate(hs, axis=1)
        cols = slice(g * GROUP_CH, (g + 1) * GROUP_CH)
        y_ref[:, cols] = _gated_group_norm(jnp.concatenate(ys, axis=1), x, zs_ref[:, cols],
                                           dskip_ref[:, cols], norm_ref[:, cols])

    @pl.when(c == N_CHUNKS - 1)
    def _():
        for g in range(N_GROUPS):
            for i in range(PAIRS_PER_GROUP):
                pair = g * PAIRS_PER_GROUP + i
                h_pair = h_s[g, :, i * LANES:(i + 1) * LANES].T
                hout_ref[0, 0, 2 * pair:2 * pair + 2] = h_pair.reshape(2, HEAD_DIM, D_STATE)


def _split3(v):
    hi = v.astype(BF16)
    r1 = v - hi.astype(F32)
    mid = r1.astype(BF16)
    lo = (r1 - mid.astype(F32)).astype(BF16)
    return hi, mid, lo


def _ssd_sample_step(g, xh_ref, xst_ref, bm_ref, bst_ref, cm_ref, cst_ref, zs_ref, dt_ref, dtb_ref, alog_ref,
                     cwx_ref, cbx_ref, cwb_ref, cbb_ref, cwc_ref, cbc_ref, dskip_ref, norm_ref, h0_ref,
                     y_ref, hout_ref):
    bb = SAMPLE_BB
    rows = DEC_SEQ * bb

    def conv_silu(cur_ref, st_ref, w_ref, b_ref):
        full = [st_ref[k] for k in range(CONV_W - 1)] + [cur_ref[t] for t in range(DEC_SEQ)]
        outs = []
        for t in range(DEC_SEQ):
            acc = b_ref[...] + full[t] * w_ref[0:1, :]
            for k in range(1, CONV_W):
                acc = acc + full[t + k] * w_ref[k:k + 1, :]
            outs.append(_silu(acc))
        return outs

    x = conv_silu(xh_ref, xst_ref, cwx_ref, cbx_ref)
    bm = conv_silu(bm_ref, bst_ref, cwb_ref, cbb_ref)
    cm = conv_silu(cm_ref, cst_ref, cwc_ref, cbc_ref)

    a = -jnp.exp(alog_ref[...])
    dt = [jax.nn.softplus(dt_ref[t] + dtb_ref[...]) for t in range(DEC_SEQ)]
    s = [dt[0] * a]
    for t in range(1, DEC_SEQ):
        s.append(s[t - 1] + dt[t] * a)
    sel = (lax.broadcasted_iota(jnp.int32, (N_HEADS, GROUP_CH), 0)
           == g * HEADS_PER_GROUP + lax.broadcasted_iota(jnp.int32, (N_HEADS, GROUP_CH), 1) // HEAD_DIM)
    sel = jnp.where(sel, 1.0, 0.0).astype(BF16)

    def expand(v):
        return sum(jnp.dot(p, sel, preferred_element_type=F32) for p in _split3(v))

    dt_e = expand(jnp.concatenate(dt, axis=0))
    s_e = expand(jnp.concatenate(s, axis=0))
    dt_e = [dt_e[t * bb:(t + 1) * bb] for t in range(DEC_SEQ)]
    s_e = [s_e[t * bb:(t + 1) * bb] for t in range(DEC_SEQ)]

    c_all = jnp.concatenate(cm, axis=0).astype(BF16)
    b_all = jnp.concatenate(bm, axis=0)
    last = DEC_SEQ - 1
    xw_all = jnp.concatenate([x[t] * dt_e[t] * jnp.exp(s_e[last] - s_e[t]) for t in range(DEC_SEQ)], axis=0)
    xw_t = xw_all.T.astype(BF16)
    dec_t = jnp.concatenate([jnp.exp(s_e[last])] + [jnp.zeros((bb, GROUP_CH), F32)] * last, axis=0).T
    row_b = lax.broadcasted_iota(jnp.int32, (rows, 1), 0) % bb
    inter = jnp.zeros((rows, GROUP_CH), F32)
    for b in range(bb):
        mine = row_b == b
        h0 = h0_ref[0, b].reshape(GROUP_CH, D_STATE)
        yb = lax.dot_general(c_all, h0.astype(BF16), (((1,), (1,)), ((), ())), preferred_element_type=F32)
        inter = inter + jnp.where(mine, yb, 0.0)
        dh = jnp.dot(xw_t, jnp.where(mine, b_all, 0.0).astype(BF16), preferred_element_type=F32)
        h_new = h0 * jnp.broadcast_to(dec_t[:, b:b + 1], (GROUP_CH, D_STATE)) + dh
        hout_ref[0, b] = h_new.reshape(HEADS_PER_GROUP, HEAD_DIM, D_STATE)

    for t in range(DEC_SEQ):
        y = inter[t * bb:(t + 1) * bb] * jnp.exp(s_e[t])
        for u in range(t + 1):
            cb = jnp.sum(cm[t] * bm[u], axis=-1, keepdims=True)
            y = y + cb * jnp.exp(s_e[t] - s_e[u]) * dt_e[u] * x[u]
        y_ref[t] = _gated_group_norm(y, x[t], zs_ref[t], dskip_ref[...], norm_ref[...])


N_PROMPT_SSD_IN = 10
N_SAMPLE_SSD_IN = 19
SSD_STEPS = BATCH * N_CHUNKS
SAMPLE_SSD_STEPS = (DEC_BATCH // SAMPLE_BB) * N_GROUPS
SAMPLE_EVERY = SSD_STEPS // SAMPLE_SSD_STEPS


def _ssd_kernel(*refs):
    p_in = refs[:N_PROMPT_SSD_IN]
    s_in = refs[N_PROMPT_SSD_IN:N_PROMPT_SSD_IN + N_SAMPLE_SSD_IN]
    yp_ref, hp_ref, ys_ref, hs_ref, h_s, xa_ref = refs[N_PROMPT_SSD_IN + N_SAMPLE_SSD_IN:]
    s = pl.program_id(0)
    _ssd_prompt_step(s % N_CHUNKS, *p_in, yp_ref, hp_ref, h_s, xa_ref)

    @pl.when(s % SAMPLE_EVERY == SAMPLE_EVERY - 1)
    def _():
        _ssd_sample_step((s // SAMPLE_EVERY) % N_GROUPS, *s_in, ys_ref, hs_ref)


def _ssd(xbc, zs, dt_t, dt_bias_col, a_log_col, xbc_s, conv_st, zs_s, dt_s, dt_bias_row, a_log_row,
         conv_w, conv_b, dskip_e, ssm_norm, h0):
    assert SSD_STEPS % SAMPLE_SSD_STEPS == 0
    bb = SAMPLE_BB
    per = CHUNK // CONV_HALO

    def full(rows, width):
        return pl.BlockSpec((rows, width), lambda s: (0, 0))

    prompt_specs = [pl.BlockSpec((CHUNK, CONV_DIM), lambda s: (s, 0)),
                    pl.BlockSpec((CONV_HALO, CONV_DIM), lambda s: (jnp.maximum(s * per - 1, 0), 0)),
                    pl.BlockSpec((CHUNK, D_INNER), lambda s: (s, 0)),
                    pl.BlockSpec((N_HEADS, CHUNK), lambda s: (0, s)),
                    full(N_HEADS, 1), full(N_HEADS, 1),
                    full(CONV_W, CONV_DIM), full(1, CONV_DIM), full(1, D_INNER), full(1, D_INNER)]

    def blk(s):
        return (s // SAMPLE_EVERY) // N_GROUPS

    def grp(s):
        return (s // SAMPLE_EVERY) % N_GROUPS

    def cur(width, col0):
        return pl.BlockSpec((DEC_SEQ, bb, width), lambda s: (SAMPLE_ROW_BLOCK, blk(s), col0 + grp(s)))

    def st(width, col0):
        return pl.BlockSpec((CONV_W - 1, bb, width), lambda s: (0, blk(s), col0 + grp(s)))

    def par(rows, width, col0):
        return pl.BlockSpec((rows, width), lambda s: (0, col0 + grp(s)))

    head_row = full(1, N_HEADS)
    state = pl.BlockSpec((1, bb, HEADS_PER_GROUP, HEAD_DIM, D_STATE), lambda s: (0, blk(s), grp(s), 0, 0))
    sample_specs = [cur(GROUP_CH, 0), st(GROUP_CH, 0),
                    cur(D_STATE, XBC_B_COL), st(D_STATE, XBC_B_COL),
                    cur(D_STATE, XBC_C_COL), st(D_STATE, XBC_C_COL),
                    cur(GROUP_CH, 0),
                    pl.BlockSpec((DEC_SEQ, bb, N_HEADS), lambda s: (SAMPLE_ROW_BLOCK, blk(s), 0)),
                    head_row, head_row,
                    par(CONV_W, GROUP_CH, 0), par(1, GROUP_CH, 0),
                    par(CONV_W, D_STATE, XBC_B_COL), par(1, D_STATE, XBC_B_COL),
                    par(CONV_W, D_STATE, XBC_C_COL), par(1, D_STATE, XBC_C_COL),
                    par(1, GROUP_CH, 0), par(1, GROUP_CH, 0), state]
    assert len(prompt_specs) == N_PROMPT_SSD_IN and len(sample_specs) == N_SAMPLE_SSD_IN
    return pl.pallas_call(
        _ssd_kernel,
        grid=(SSD_STEPS,),
        in_specs=prompt_specs + sample_specs,
        out_specs=[pl.BlockSpec((CHUNK, D_INNER), lambda s: (s, 0)),
                   pl.BlockSpec((1, 1, N_HEADS, HEAD_DIM, D_STATE), lambda s: (0, s // N_CHUNKS, 0, 0, 0)),
                   pl.BlockSpec((DEC_SEQ, bb, GROUP_CH), lambda s: (0, blk(s), grp(s))),
                   state],
        out_shape=[jax.ShapeDtypeStruct((N_PROMPT, D_INNER), BF16),
                   jax.ShapeDtypeStruct((1, BATCH, N_HEADS, HEAD_DIM, D_STATE), F32),
                   jax.ShapeDtypeStruct((DEC_SEQ, DEC_BATCH, D_INNER), BF16),
                   jax.ShapeDtypeStruct((1, DEC_BATCH, N_HEADS, HEAD_DIM, D_STATE), F32)],
        scratch_shapes=[pltpu.VMEM((N_GROUPS, D_STATE, GROUP_CH), F32), pltpu.VMEM((CHUNK, CONV_DIM), BF16)],
        compiler_params=_params(1, 60),
        name="ssd_mixers",
    )(xbc, xbc, zs, dt_t, dt_bias_col, a_log_col, conv_w, conv_b, dskip_e, ssm_norm,
      xbc_s, conv_st, xbc_s, conv_st, xbc_s, conv_st, zs_s, dt_s, dt_bias_row, a_log_row,
      conv_w, conv_b, conv_w, conv_b, conv_w, conv_b, dskip_e, ssm_norm, h0)


def kernel(x_prompt, x_sample, c_prompt, c_sample, state_ssm, state_conv, state_pool, w_ada, b_ada, norm_ffn1,
           w13_ffn1, w2_ffn1, norm_mix, w_in, pool_w, pool_scale, conv_w, conv_b, dt_bias, a_log, d_skip,
           ssm_norm, w_branch_pool, w_branch_ssm, w_out, norm_ffn2, w13_ffn2, w2_ffn2, norm_final):
    d = D_MODEL

    def layer0(w):
        return w.reshape(w.shape[1:])

    def row(v):
        return v.reshape(1, -1)

    xp = x_prompt.reshape(N_PROMPT, d)
    xs = x_sample.transpose(1, 0, 2).reshape(N_SAMPLE, d)

    n_c = BATCH + DEC_BATCH
    c_all = jnp.pad(jnp.concatenate([c_prompt, c_sample], axis=0), ((0, -n_c % 8), (0, 0)))
    mods = _ada(c_all, layer0(w_ada), b_ada.reshape(1, N_MOD * d))
    mods_p = mods[:BATCH]
    mods_s = mods[BATCH:n_c]

    a1 = _normmod(xp, xs, row(norm_ffn1), mods_p, mods_s, 0, 1)
    act1 = _ffn_up(a1, layer0(w13_ffn1))
    h1, u = _res_block(act1, layer0(w2_ffn1), (xp, xs), mods_p, mods_s, 2, 0.5, row(norm_mix), (3, 4),
                       tm=256, chunk=176, vmem_mib=58)

    w_in_t = layer0(w_in).T
    conv_buf = layer0(state_conv)
    u_pool, zs, xbc, dt_raw, gates = _in_proj(u, w_in_t)
    conv_w0 = layer0(conv_w)
    conv_b0 = row(conv_b)

    pw = layer0(pool_w)
    ps = row(pool_scale)
    u_pool_s = u_pool[N_PROMPT:].reshape(DEC_SEQ, DEC_BATCH, d)
    pool_buf = layer0(state_pool)
    y_pool_p = _pool_prompt(u_pool, pw, ps)
    y_pool_s = _pool_sample(u_pool.reshape(ROW_SLABS, DEC_BATCH, d), pool_buf.transpose(1, 0, 2),
                            pw, ps).reshape(N_SAMPLE, d)

    dskip_e = row(jnp.repeat(d_skip.reshape(N_HEADS), HEAD_DIM))
    norm_row = row(ssm_norm)
    y_ssm_p, prompt_ssm, y_ssm_s, sample_ssm = _ssd(
        xbc, zs, dt_raw[:N_PROMPT].T, dt_bias.reshape(N_HEADS, 1), a_log.reshape(N_HEADS, 1),
        xbc.reshape(ROW_SLABS, DEC_BATCH, CONV_DIM), conv_buf.transpose(1, 0, 2),
        zs.reshape(ROW_SLABS, DEC_BATCH, D_INNER), dt_raw.reshape(ROW_SLABS, DEC_BATCH, N_HEADS),
        row(dt_bias), row(a_log), conv_w0, conv_b0, dskip_e, norm_row, state_ssm)

    merged = _merge(y_pool_p, y_pool_s, y_ssm_p, y_ssm_s.reshape(N_SAMPLE, D_INNER), gates,
                    layer0(w_branch_pool), layer0(w_branch_ssm))
    h2, a3 = _res_block(merged, layer0(w_out), h1, mods_p, mods_s, 5, 1.0, row(norm_ffn2), (6, 7),
                        tm=512, chunk=256, vmem_mib=52)
    act2 = _ffn_up(a3, layer0(w13_ffn2))
    y_p, y_s = _res_block(act2, layer0(w2_ffn2), h2, mods_p, mods_s, 8, 0.5, row(norm_final), None,
                          tm=256, chunk=176, vmem_mib=56)

    y_prompt = y_p.reshape(BATCH, SEQ, d)
    y_sample = y_s.reshape(DEC_SEQ, DEC_BATCH, d).transpose(1, 0, 2)
    keep = CONV_W - 1
    seq_end = (jnp.arange(BATCH) + 1) * SEQ
    conv_rows = (seq_end[:, None] - keep + jnp.arange(keep)[None, :]).reshape(-1)
    pool_rows = (seq_end[:, None] - POOL_BUF + jnp.arange(POOL_BUF)[None, :]).reshape(-1)
    prompt_conv = jnp.take(xbc, conv_rows, axis=0).reshape(1, BATCH, keep, CONV_DIM)
    prompt_pool = jnp.take(u_pool, pool_rows, axis=0).reshape(1, BATCH, POOL_BUF, d)
    assert DEC_SEQ >= keep
    xbc_last = xbc[N_ROWS - keep * DEC_BATCH:].reshape(keep, DEC_BATCH, CONV_DIM)
    sample_conv = xbc_last.transpose(1, 0, 2)[None]
    sample_pool = jnp.concatenate([pool_buf, u_pool_s.transpose(1, 0, 2)], axis=1)[:, -POOL_BUF:][None]
    return (y_prompt, y_sample, prompt_ssm, prompt_conv, prompt_pool, sample_ssm, sample_conv, sample_pool)
```
